```python
import jax
import jax.numpy as jnp
from jax import lax
import numpy as np


D_MODEL = 4096
BATCH = 2
SEQ = 8192
DEPTH = 2

CTX_LEN = 256
GRID_W = 64

A_HEAD_DIM = 128
A_HEADS = (D_MODEL // 2) // A_HEAD_DIM
A_KV_HEADS = 4
A_WINDOW = 128
A_BLOCK = 128
ROPE_THETA = 10000.0
B_WIDTH = D_MODEL // 2
B_HEAD_DIM = 64
B_HEADS = B_WIDTH // B_HEAD_DIM
DECAY_LORA = 96
AAA_LORA = 96
GN_EPS = 64e-5
C_WIDTH = 3 * D_MODEL
C_CHUNK = 128
C_GROUPS = 16
LN_EPS = 1e-5
NORM_EPS = 1e-6

A_Q = A_HEADS * A_HEAD_DIM
A_KV = A_KV_HEADS * A_HEAD_DIM
AB_WIDTHS = (A_Q, A_KV, A_KV, A_Q, B_WIDTH, B_WIDTH, B_WIDTH, B_WIDTH, 2 * DECAY_LORA, 2 * AAA_LORA)
AB_IN = sum(AB_WIDTHS)
AB_SPLITS = tuple(int(s) for s in np.cumsum(AB_WIDTHS)[:-1])
AB_MIX = A_Q + B_WIDTH
N_EVEN = (DEPTH + 1) // 2
N_ODD = DEPTH // 2

kernel_name = 'hybrid_swa_rwkv7_gmlp_dit_block'


def rms_norm(x, g):
    xf = x.astype(jnp.float32)
    y = xf * lax.rsqrt(jnp.mean(xf * xf, axis=-1, keepdims=True) + NORM_EPS)
    return (y * g.astype(jnp.float32)).astype(x.dtype)


def layer_norm(x, g, b, eps):
    xf = x.astype(jnp.float32)
    xc = xf - jnp.mean(xf, axis=-1, keepdims=True)
    var = jnp.mean(xc * xc, axis=-1, keepdims=True)
    return xc * lax.rsqrt(var + eps) * g.astype(jnp.float32) + b.astype(jnp.float32)


def centred_conv3(x, w):
    xp = jnp.pad(x, ((0, 0), (1, 1), (0, 0)))
    return xp[:, :-2] * w[0] + xp[:, 1:-1] * w[1] + xp[:, 2:] * w[2]


def axial_rope_angles(n_tok):
    rows = n_tok // GRID_W
    row = jnp.repeat(jnp.arange(rows, dtype=jnp.float32), GRID_W)
    col = jnp.tile(jnp.arange(GRID_W, dtype=jnp.float32), rows)
    n_freq = A_HEAD_DIM // 4
    inv_freq = ROPE_THETA ** (-jnp.arange(n_freq, dtype=jnp.float32) / n_freq)
    return row[:, None] * inv_freq, col[:, None] * inv_freq


def rotate_half_pairs(x, ang):
    cos = jnp.cos(ang)[None, :, None, :].astype(x.dtype)
    sin = jnp.sin(ang)[None, :, None, :].astype(x.dtype)
    x1, x2 = jnp.split(x, 2, axis=-1)
    return jnp.concatenate([x1 * cos - x2 * sin, x2 * cos + x1 * sin], axis=-1)


def axial_rope(x, ang_row, ang_col):
    x_row, x_col = jnp.split(x, 2, axis=-1)
    return jnp.concatenate([rotate_half_pairs(x_row, ang_row), rotate_half_pairs(x_col, ang_col)], axis=-1)


def band_context_attention(q, k, v, kc, vc, sink):
    B, T, HQ, Dh = q.shape
    HK = k.shape[2]
    G = HQ // HK
    nb = T // A_BLOCK
    f32 = jnp.float32
    qb = q.reshape(B, nb, A_BLOCK, HK, G, Dh).astype(f32) * (Dh ** -0.5)

    def windows(t):
        tp = jnp.pad(t, ((0, 0), (A_BLOCK, A_BLOCK), (0, 0), (0, 0))).reshape(B, nb + 2, A_BLOCK, HK, Dh)
        return jnp.concatenate([tp[:, i:i + nb] for i in range(3)], axis=2).astype(f32)

    kw, vw = windows(k), windows(v)
    qpos = jnp.arange(nb)[:, None, None] * A_BLOCK + jnp.arange(A_BLOCK)[None, :, None]
    kpos = (jnp.arange(nb)[:, None, None] - 1) * A_BLOCK + jnp.arange(3 * A_BLOCK)[None, None, :]
    valid = (jnp.abs(qpos - kpos) <= A_WINDOW) & (kpos >= 0) & (kpos < T)
    s_band = jnp.einsum('bnqhgd,bnkhd->bhgnqk', qb, kw)
    s_band = jnp.where(valid, s_band, -jnp.inf)
    s_ctx = jnp.einsum('bnqhgd,blhd->bhgnql', qb, kc.astype(f32))
    s_sink = sink.astype(f32).reshape(1, HK, G, 1, 1, 1)
    m = jnp.maximum(jnp.maximum(s_band.max(-1, keepdims=True), s_ctx.max(-1, keepdims=True)), s_sink)
    p_band = jnp.exp(s_band - m)
    p_ctx = jnp.exp(s_ctx - m)
    den = jnp.exp(s_sink - m) + p_band.sum(-1, keepdims=True) + p_ctx.sum(-1, keepdims=True)
    o = (jnp.einsum('bhgnqk,bnkhd->bnqhgd', p_band / den, vw)
         + jnp.einsum('bhgnql,blhd->bnqhgd', p_ctx / den, vc.astype(f32)))
    return o.reshape(B, T, HQ * Dh).astype(q.dtype)


def context_attention(q, k, v, sink):
    B, L, HQ, Dh = q.shape
    HK = k.shape[2]
    G = HQ // HK
    f32 = jnp.float32
    qf = q.reshape(B, L, HK, G, Dh).astype(f32) * (Dh ** -0.5)
    s = jnp.einsum('blhgd,bmhd->bhglm', qf, k.astype(f32))
    s_sink = jnp.broadcast_to(sink.astype(f32).reshape(1, HK, G, 1, 1), s.shape[:-1] + (1,))
    p = jax.nn.softmax(jnp.concatenate([s_sink, s], axis=-1), axis=-1)[..., 1:]
    o = jnp.einsum('bhglm,bmhd->blhgd', p, v.astype(f32))
    return o.reshape(B, L, HQ * Dh).astype(q.dtype)


def rwkv_prep(z_r, z_k, z_v, z_w, z_a, conv_w, w0, w2, a0, a2, k_k, k_a):
    f32 = jnp.float32
    B, T, C = z_r.shape
    rkv = centred_conv3(jnp.concatenate([z_r, z_k, z_v], axis=-1), conv_w).astype(f32)
    r, k, v = jnp.split(rkv, 3, axis=-1)

    def heads(t):
        return t.reshape(t.shape[:-1] + (B_HEADS, B_HEAD_DIM))

    kk = heads(k * k_k)
    kk = kk / jnp.maximum(jnp.sqrt(jnp.sum(kk * kk, axis=-1, keepdims=True)), 1e-12)
    lw = jnp.tanh(z_w.astype(f32).reshape(B, T, 2, DECAY_LORA))
    la = z_a.astype(f32).reshape(B, T, 2, AAA_LORA)
    w_log = -jax.nn.softplus(-(w0 + jnp.einsum('btdr,drc->btdc', lw, w2))) - 0.5
    decay = jnp.exp(-jnp.exp(w_log))
    a = jax.nn.sigmoid(a0 + jnp.einsum('btdr,drc->btdc', la, a2))
    k_dir = k[:, :, None] * (1.0 + (a - 1.0) * k_a)
    return heads(r), heads(k), heads(v), kk, heads(decay), heads(k_dir), heads(a)


def wkv_scan(state, w, k, v, kk, a, r=None, reverse=False):
    def tm(t):
        return jnp.swapaxes(t, 0, 1)

    def update(S, w_t, k_t, v_t, kk_t, a_t):
        sk = jnp.einsum('bhvk,bhk->bhv', S, kk_t)
        return (S * w_t[:, :, None, :] - sk[..., None] * (kk_t * a_t)[:, :, None, :]
                + v_t[..., None] * k_t[:, :, None, :])

    xs = tuple(tm(t) for t in (w, k, v, kk, a))
    if r is None:
        S, _ = lax.scan(lambda S, inp: (update(S, *inp), None), state, xs, reverse=reverse)
        return S, None

    def step(S, inp):
        S = update(S, *inp[:5])
        return S, jnp.einsum('bhvk,bhk->bhv', S, inp[5])

    S, ys = lax.scan(step, state, xs + (tm(r),), reverse=reverse)
    return S, tm(ys)


def rwkv_readout(y, r, k, v, r_k, gn_w, gn_b):
    y = layer_norm(y, gn_w.reshape(B_HEADS, B_HEAD_DIM), gn_b.reshape(B_HEADS, B_HEAD_DIM), GN_EPS)
    y = y + jnp.sum(r * k * r_k.reshape(B_HEADS, B_HEAD_DIM).astype(jnp.float32), axis=-1, keepdims=True) * v
    return y.reshape(y.shape[:2] + (B_WIDTH,))


def ab_mixer(h_lat, h_ctx, ang_row, ang_col, w_in, w_out, sink, conv_w, w0, w2, a0, a2,
             k_k, k_a, r_k, gn_w, gn_b, ctx_out):
    zl = jnp.split(h_lat @ w_in, AB_SPLITS, axis=-1)
    zc = jnp.split(h_ctx @ w_in, AB_SPLITS, axis=-1)

    def ah(t):
        return t.reshape(t.shape[:2] + (-1, A_HEAD_DIM))

    q_l = axial_rope(ah(zl[0]), ang_row, ang_col)
    k_l = axial_rope(ah(zl[1]), ang_row, ang_col)
    v_l = ah(zl[2])
    k_c, v_c = ah(zc[1]), ah(zc[2])
    o_a = band_context_attention(q_l, k_l, v_l, k_c, v_c, sink) * jax.nn.silu(zl[3])

    r_l, kb_l, vb_l, kk_l, dec_l, kd_l, a_l = rwkv_prep(zl[4], zl[5], zl[6], zl[8], zl[9],
                                                        conv_w, w0, w2, a0, a2, k_k, k_a)
    r_c, kb_c, vb_c, kk_c, dec_c, kd_c, a_c = rwkv_prep(zc[4], zc[5], zc[6], zc[8], zc[9],
                                                        conv_w, w0, w2, a0, a2, k_k, k_a)
    B = h_lat.shape[0]
    S0 = jnp.zeros((B, B_HEADS, B_HEAD_DIM, B_HEAD_DIM), jnp.float32)
    ys_l, ys_c = [], []
    for d, rev in enumerate((False, True)):
        S_c, y_c = wkv_scan(S0, dec_c[:, :, d], kd_c[:, :, d], vb_c, kk_c, a_c[:, :, d],
                            r_c if ctx_out else None, reverse=rev)
        _, y_l = wkv_scan(S_c, dec_l[:, :, d], kd_l[:, :, d], vb_l, kk_l, a_l[:, :, d], r_l, reverse=rev)
        ys_l.append(y_l)
        ys_c.append(y_c)
    o_b = rwkv_readout(ys_l[0] + ys_l[1], r_l, kb_l, vb_l, r_k, gn_w, gn_b).astype(h_lat.dtype)
    o_b = o_b * jax.nn.silu(zl[7])
    out_l = jnp.concatenate([o_a, o_b], axis=-1) @ w_out
    if not ctx_out:
        return out_l, None
    o_ac = context_attention(ah(zc[0]), k_c, v_c, sink) * jax.nn.silu(zc[3])
    o_bc = rwkv_readout(ys_c[0] + ys_c[1], r_c, kb_c, vb_c, r_k, gn_w, gn_b).astype(h_ctx.dtype)
    o_bc = o_bc * jax.nn.silu(zc[7])
    out_c = jnp.concatenate([o_ac, o_bc], axis=-1) @ w_out
    return out_l, out_c


def gmlp_branch(h, w_in, ln_g, ln_b, w_s, b_s, w_out):
    B, T, _ = h.shape
    nc = T // C_CHUNK
    u, v, g = jnp.split(h @ w_in, 3, axis=-1)
    u = jax.nn.gelu(u)
    v = layer_norm(jax.nn.gelu(v), ln_g, ln_b, LN_EPS).astype(h.dtype)
    vb = v.reshape(B, nc, C_CHUNK, C_GROUPS, C_WIDTH // C_GROUPS)
    vm = jnp.einsum('gij,bnjgc->bnigc', w_s, vb) + b_s.T[:, :, None]
    y = u * vm.reshape(B, T, C_WIDTH) * jax.nn.silu(g)
    return y @ w_out


def setup_inputs(seed: int = 0) -> dict:
    key = jax.random.key(seed)
    keys = iter(jax.random.split(key, 40))

    def nrm(shape, std):
        return jax.random.normal(next(keys), shape, jnp.float32) * std

    D = D_MODEL
    conv_base = jnp.array([0.25, 0.5, 0.25], jnp.float32)[None, :, None]
    return {
        'x': nrm((BATCH, SEQ, D), 1.0),
        'c': nrm((BATCH, D), 1.0),
        'ctx': nrm((BATCH, CTX_LEN, D), 1.0),
        'c_ctx': nrm((D,), 1.0),
        'mod_w': nrm((DEPTH, D, 3 * D), 0.5 * D ** -0.5),
        'mod_b': nrm((DEPTH, 3 * D), 0.02),
        'norm_g': 1.0 + nrm((DEPTH, D), 0.02),
        'ab_w_in': nrm((N_EVEN, D, AB_IN), D ** -0.5),
        'ab_w_out': nrm((N_EVEN, AB_MIX, D), AB_MIX ** -0.5),
        'attn_sink': nrm((N_EVEN, A_HEADS), 1.0),
        'rwkv_conv': conv_base + nrm((N_EVEN, 3, 3 * B_WIDTH), 0.1),
        'rwkv_w0': -1.0 + nrm((N_EVEN, 2, B_WIDTH), 1.0),
        'rwkv_w2': nrm((N_EVEN, 2, DECAY_LORA, B_WIDTH), 0.5 * DECAY_LORA ** -0.5),
        'rwkv_a0': nrm((N_EVEN, 2, B_WIDTH), 0.5),
        'rwkv_a2': nrm((N_EVEN, 2, AAA_LORA, B_WIDTH), 0.5 * AAA_LORA ** -0.5),
        'rwkv_k_k': 0.85 + nrm((N_EVEN, B_WIDTH), 0.05),
        'rwkv_k_a': 1.0 + nrm((N_EVEN, B_WIDTH), 0.05),
        'rwkv_r_k': nrm((N_EVEN, B_WIDTH), 0.1),
        'rwkv_gn_w': 1.0 + nrm((N_EVEN, B_WIDTH), 0.02),
        'rwkv_gn_b': nrm((N_EVEN, B_WIDTH), 0.02),
        'gm_w_in': nrm((N_ODD, D, 3 * C_WIDTH), D ** -0.5),
        'gm_ln_g': 1.0 + nrm((N_ODD, C_WIDTH), 0.02),
        'gm_ln_b': nrm((N_ODD, C_WIDTH), 0.02),
        'gm_w_s': nrm((N_ODD, C_GROUPS, C_CHUNK, C_CHUNK), C_CHUNK ** -0.5),
        'gm_b_s': 1.0 + nrm((N_ODD, C_GROUPS, C_CHUNK), 0.02),
        'gm_w_out': nrm((N_ODD, C_WIDTH, D), C_WIDTH ** -0.5),
        'final_g': 1.0 + nrm((D,), 0.02),
    }


def reference(x, c, ctx, c_ctx, mod_w, mod_b, norm_g, ab_w_in, ab_w_out, attn_sink, rwkv_conv,
              rwkv_w0, rwkv_w2, rwkv_a0, rwkv_a2, rwkv_k_k, rwkv_k_a, rwkv_r_k, rwkv_gn_w, rwkv_gn_b,
              gm_w_in, gm_ln_g, gm_ln_b, gm_w_s, gm_b_s, gm_w_out, final_g):
    n_lat = x.shape[1]
    ang_row, ang_col = axial_rope_angles(n_lat)
    cond_lat = jax.nn.silu(c)[:, None, :]
    cond_ctx = jax.nn.silu(c_ctx)
    for l in range(DEPTH):
        i = l // 2
        ctx_read_later = any(j % 2 == 0 for j in range(l + 1, DEPTH))
        shift, scale, gate = jnp.split(cond_lat @ mod_w[l] + mod_b[l], 3, axis=-1)
        h = rms_norm(x, norm_g[l]) * (1.0 + scale) + shift
        if l % 2 == 0 or ctx_read_later:
            shift_c, scale_c, gate_c = jnp.split(cond_ctx @ mod_w[l] + mod_b[l], 3, axis=-1)
            h_c = rms_norm(ctx, norm_g[l]) * (1.0 + scale_c) + shift_c
        if l % 2 == 0:
            y, y_c = ab_mixer(h, h_c, ang_row, ang_col, ab_w_in[i], ab_w_out[i], attn_sink[i],
                              rwkv_conv[i], rwkv_w0[i], rwkv_w2[i], rwkv_a0[i], rwkv_a2[i],
                              rwkv_k_k[i], rwkv_k_a[i], rwkv_r_k[i], rwkv_gn_w[i], rwkv_gn_b[i],
                              ctx_read_later)
        else:
            y = gmlp_branch(h, gm_w_in[i], gm_ln_g[i], gm_ln_b[i], gm_w_s[i], gm_b_s[i], gm_w_out[i])
            y_c = (gmlp_branch(h_c, gm_w_in[i], gm_ln_g[i], gm_ln_b[i], gm_w_s[i], gm_b_s[i], gm_w_out[i])
                   if ctx_read_later else None)
        x = x + gate * y
        if ctx_read_later:
            ctx = ctx + gate_c * y_c
    return rms_norm(x, final_g)
```

```python
import functools

import jax
import jax.numpy as jnp
from jax import lax
from jax.experimental import pallas as pl
from jax.experimental.pallas import tpu as pltpu

F32 = jnp.float32
BF16 = jnp.bfloat16

A_HEAD_DIM = 128
A_WINDOW = 128
A_BLOCK = 128
GRID_W = 64
ROPE_THETA = 10000.0
B_HEAD_DIM = 64
LORA_RANK = 96
GN_EPS = 64e-5
C_CHUNK = 128
LN_EPS = 1e-5
NORM_EPS = 1e-6

LANES = 128
SUBLANES = 8
VMEM_LIMIT_BYTES = 56 * 1024 * 1024

SCAN_CHUNK = 64
SCAN_LEVELS = 6
SCAN_GROUP_WIDTH = 512
LORA_SLOT = 256
MASK_VALUE = -1e30


def _params(*semantics):
    return pltpu.CompilerParams(dimension_semantics=semantics, vmem_limit_bytes=VMEM_LIMIT_BYTES)


def _pick(n, candidates):
    for c in candidates:
        if n % c == 0:
            return c
    raise ValueError(f"no tile in {candidates} divides {n}")


def _split(x, terms):
    parts = []
    rest = x
    for i in range(terms):
        p = rest.astype(BF16)
        parts.append(p)
        if i + 1 < terms:
            rest = rest - p.astype(F32)
    return parts


_NN = (((1,), (0,)), ((), ()))
_NT = (((1,), (1,)), ((), ()))
_TN = (((0,), (0,)), ((), ()))


def _dotp(a, b, dims=_NN, ta=1, tb=1):
    pa = _split(a, ta)
    pb = _split(b, tb)
    out = None
    for i in range(ta):
        for j in range(tb):
            if i + j >= max(ta, tb):
                continue
            t = lax.dot_general(pa[i], pb[j], dims, preferred_element_type=F32)
            out = t if out is None else out + t
    return out


def _silu(x):
    return x * jax.nn.sigmoid(x)


def _mod_kernel(c_ref, w_ref, b_ref, o_ref):
    s = _silu(c_ref[...])
    o_ref[...] = _dotp(s, w_ref[...], ta=2, tb=2) + b_ref[...]


def _modulation(cvec, mod_w, mod_b):
    depth, d, n = mod_w.shape
    rows = cvec.shape[0]
    tn = _pick(n, (512, 256, 128))
    return pl.pallas_call(
        _mod_kernel,
        grid=(depth, n // tn),
        in_specs=[
            pl.BlockSpec((rows, d), lambda l, j: (0, 0)),
            pl.BlockSpec((None, d, tn), lambda l, j: (l, 0, j)),
            pl.BlockSpec((None, 1, tn), lambda l, j: (l, 0, j)),
        ],
        out_specs=pl.BlockSpec((None, rows, tn), lambda l, j: (l, 0, j)),
        out_shape=jax.ShapeDtypeStruct((depth, rows, n), F32),
        compiler_params=_params("parallel", "parallel"),
        name="modulation",
    )(cvec, mod_w, mod_b.reshape(depth, 1, n))


def _norm_rows(x, g, shift, scale):
    ms = jnp.mean(x * x, axis=-1, keepdims=True)
    y = x * lax.rsqrt(ms + NORM_EPS)
    return (y * g) * (1.0 + scale) + shift


def _norm_mod_kernel(x_ref, c_ref, g_ref, shift_ref, scale_ref, o_ref, *, ctx_blocks):
    j = pl.program_id(1)

    @pl.when(j < ctx_blocks)
    def _():
        o_ref[...] = _norm_rows(c_ref[...], g_ref[...], shift_ref[...], scale_ref[...]).astype(BF16)

    @pl.when(j >= ctx_blocks)
    def _():
        o_ref[...] = _norm_rows(x_ref[...], g_ref[...], shift_ref[...], scale_ref[...]).astype(BF16)


def _norm_mod(x, ctx, g, mod, ctx_row):
    bsz, t, d = x.shape
    tm = 256
    if ctx is None:
        ctx = x
        cb = 0
    else:
        cb = ctx.shape[1] // tm
    nblk = cb + t // tm
    rows = mod.shape[0]
    mod3 = mod.reshape(rows, 1, 3 * d)

    def mrow(b, j):
        return jnp.where(j < cb, ctx_row, b)

    return pl.pallas_call(
        functools.partial(_norm_mod_kernel, ctx_blocks=cb),
        grid=(bsz, nblk),
        in_specs=[
            pl.BlockSpec((None, tm, d), lambda b, j: (b, jnp.maximum(j - cb, 0), 0)),
            pl.BlockSpec((None, tm, d), lambda b, j: (b, jnp.minimum(j, max(cb - 1, 0)), 0)),
            pl.BlockSpec((1, d), lambda b, j: (0, 0)),
            pl.BlockSpec((None, 1, d), lambda b, j: (mrow(b, j), 0, 0)),
            pl.BlockSpec((None, 1, d), lambda b, j: (mrow(b, j), 0, 1)),
        ],
        out_specs=pl.BlockSpec((None, tm, d), lambda b, j: (b, j, 0)),
        out_shape=jax.ShapeDtypeStruct((bsz, nblk * tm, d), BF16),
        compiler_params=_params("parallel", "parallel"),
        name="norm_mod",
    )(x, ctx, g.reshape(1, d), mod3, mod3)


def _mm_kernel(a_ref, b_ref, o_ref):
    o_ref[...] = jnp.dot(a_ref[...], b_ref[...], preferred_element_type=F32)


def _matmul(a, b):
    m, k = a.shape
    n = b.shape[1]
    tm = _pick(m, (768, 1024, 512, 256))
    tn = _pick(n, (512, 256, 128))
    return pl.pallas_call(
        _mm_kernel,
        grid=(m // tm, n // tn),
        in_specs=[
            pl.BlockSpec((tm, k), lambda i, j: (i, 0)),
            pl.BlockSpec((k, tn), lambda i, j: (0, j)),
        ],
        out_specs=pl.BlockSpec((tm, tn), lambda i, j: (i, j)),
        out_shape=jax.ShapeDtypeStruct((m, n), F32),
        compiler_params=_params("parallel", "parallel"),
        name="in_proj",
    )(a, b)


def _rope(x, cos, sin_signed):
    lane = lax.broadcasted_iota(jnp.int32, cos.shape, 1)
    low = (lane % 64) < 32
    out = []
    for h in range(x.shape[1] // A_HEAD_DIM):
        xh = x[:, h * A_HEAD_DIM:(h + 1) * A_HEAD_DIM]
        partner = jnp.where(low, pltpu.roll(xh, A_HEAD_DIM - 32, 1), pltpu.roll(xh, 32, 1))
        out.append(xh * cos + partner * sin_signed)
    return out


def _attn_kernel(sink_ref, q_ref, g_ref, kp_ref, kc_ref, kn_ref, vp_ref, vc_ref, vn_ref,
                 kx_ref, vx_ref, cq_ref, sq_ref, cp_ref, sp_ref, cn_ref, sn_ref, o_ref,
                 *, n_blocks, kv_heads, group):
    n = pl.program_id(1)
    blk = A_BLOCK
    lctx = kx_ref.shape[0]
    scale = A_HEAD_DIM ** -0.5

    q_heads = _rope(q_ref[...], cq_ref[...], sq_ref[...])
    kp = _rope(kp_ref[...], cp_ref[...], sp_ref[...])
    kc = _rope(kc_ref[...], cq_ref[...], sq_ref[...])
    kn = _rope(kn_ref[...], cn_ref[...], sn_ref[...])

    qi = lax.broadcasted_iota(jnp.int32, (blk, lctx + 3 * blk), 0)
    kj = lax.broadcasted_iota(jnp.int32, (blk, lctx + 3 * blk), 1) - lctx
    no_prev = jnp.where(n > 0, 0, blk)
    no_next = jnp.where(n < n_blocks - 1, 0, blk)
    prev_ok = (kj >= qi + no_prev) & (kj < blk)
    next_ok = (kj - 2 * blk <= qi - no_next) & (kj >= 2 * blk)
    valid = (kj < 0) | prev_ok | ((kj >= blk) & (kj < 2 * blk)) | next_ok
    mask = jnp.where(valid, 0.0, MASK_VALUE).astype(F32)
    mask = jnp.concatenate([mask] * group, axis=0)

    gate = g_ref[...]
    for hk in range(kv_heads):
        sl = slice(hk * A_HEAD_DIM, (hk + 1) * A_HEAD_DIM)
        kcat = jnp.concatenate([kx_ref[:, sl], kp[hk], kc[hk], kn[hk]], axis=0).astype(BF16)
        vcat = jnp.concatenate([vx_ref[:, sl], vp_ref[:, sl], vc_ref[:, sl], vn_ref[:, sl]], axis=0).astype(BF16)
        qs = jnp.concatenate([q_heads[hk * group + g] * scale for g in range(group)], axis=0).astype(BF16)
        s = lax.dot_general(qs, kcat, _NT, preferred_element_type=F32) + mask
        sink = jnp.concatenate(
            [jnp.full((blk, 1), sink_ref[hk * group + g], F32) for g in range(group)], axis=0)
        m = jnp.maximum(jnp.max(s, axis=-1, keepdims=True), sink)
        p = jnp.exp(s - m)
        den = jnp.exp(sink - m) + jnp.sum(p, axis=-1, keepdims=True)
        o = jnp.dot(p.astype(BF16), vcat, preferred_element_type=F32) / den
        for g in range(group):
            h = hk * group + g
            hs = slice(h * A_HEAD_DIM, (h + 1) * A_HEAD_DIM)
            o_ref[:, hs] = (o[g * blk:(g + 1) * blk] * _silu(gate[:, hs])).astype(BF16)


def _attention(z, sink, tables, lay, n_ctx, n_lat):
    bsz = z.shape[0]
    aq, akv = lay["aq"], lay["akv"]
    heads = aq // A_HEAD_DIM
    kv_heads = akv // A_HEAD_DIM
    blk = A_BLOCK
    nb = n_lat // blk
    off = n_ctx // blk
    kcol = lay["k"] // akv
    vcol = lay["v"] // akv
    cos, sin = tables

    def prev(n):
        return jnp.maximum(n - 1, 0)

    def nxt(n):
        return jnp.minimum(n + 1, nb - 1)

    def zspec(width, rowf, col):
        return pl.BlockSpec((None, blk, width), lambda b, n: (b, rowf(n) + off, col))

    def tspec(rowf):
        return pl.BlockSpec((blk, A_HEAD_DIM), lambda b, n: (rowf(n), 0))

    ident = lambda n: n
    return pl.pallas_call(
        functools.partial(_attn_kernel, n_blocks=nb, kv_heads=kv_heads, group=heads // kv_heads),
        grid=(bsz, nb),
        in_specs=[
            pl.BlockSpec(memory_space=pltpu.SMEM),
            zspec(aq, ident, lay["q"] // aq),
            zspec(aq, ident, lay["ga"] // aq),
            zspec(akv, prev, kcol), zspec(akv, ident, kcol), zspec(akv, nxt, kcol),
            zspec(akv, prev, vcol), zspec(akv, ident, vcol), zspec(akv, nxt, vcol),
            pl.BlockSpec((None, n_ctx, akv), lambda b, n: (b, 0, kcol)),
            pl.BlockSpec((None, n_ctx, akv), lambda b, n: (b, 0, vcol)),
            tspec(ident), tspec(ident), tspec(prev), tspec(prev), tspec(nxt), tspec(nxt),
        ],
        out_specs=pl.BlockSpec((None, blk, aq), lambda b, n: (b, n, 0)),
        out_shape=jax.ShapeDtypeStruct((bsz, n_lat, aq), BF16),
        compiler_params=_params("parallel", "parallel"),
        name="band_attention",
    )(sink, z, z, z, z, z, z, z, z, z, z, cos, sin, cos, sin, cos, sin)


def _rope_tables(n_tok):
    rows = n_tok // GRID_W
    row = jnp.repeat(jnp.arange(rows, dtype=F32), GRID_W)
    col = jnp.tile(jnp.arange(GRID_W, dtype=F32), rows)
    n_freq = A_HEAD_DIM // 4
    inv_freq = ROPE_THETA ** (-jnp.arange(n_freq, dtype=F32) / n_freq)
    ar = row[:, None] * inv_freq
    ac = col[:, None] * inv_freq
    cos = jnp.concatenate([jnp.cos(ar), jnp.cos(ar), jnp.cos(ac), jnp.cos(ac)], axis=-1)
    sin = jnp.concatenate([-jnp.sin(ar), jnp.sin(ar), -jnp.sin(ac), jnp.sin(ac)], axis=-1)
    return cos, sin


def _head_sum(x, ones_bd):
    out = []
    for i in range(x.shape[1] // LANES):
        out.append(_dotp(x[:, i * LANES:(i + 1) * LANES], ones_bd, ta=3, tb=1))
    return jnp.concatenate(out, axis=1) if len(out) > 1 else out[0]


def _head_ones():
    r = lax.broadcasted_iota(jnp.int32, (LANES, LANES), 0) // B_HEAD_DIM
    c = lax.broadcasted_iota(jnp.int32, (LANES, LANES), 1) // B_HEAD_DIM
    return (r == c).astype(F32)


def _prep_kernel(zr_ref, zk_ref, zv_ref, zrp_ref, zkp_ref, zvp_ref, zrn_ref, zkn_ref, zvn_ref,
                 zwa_ref, cwr_ref, cwk_ref, cwv_ref, w2a_ref, w2b_ref, a2a_ref, a2b_ref,
                 w0_ref, a0_ref, kk_ref, ka_ref, rk_ref,
                 r_o, v_o, kk_o, bonus_o, lw_o, kd_o, kb_o, *, ctx_blocks, n_blocks):
    j = pl.program_id(1)
    tm = zr_ref.shape[0]
    has_prev = jnp.logical_and(j != 0, j != ctx_blocks)
    has_next = jnp.logical_and(j != ctx_blocks - 1, j != n_blocks - 1)
    rows = lax.broadcasted_iota(jnp.int32, zr_ref.shape, 0)

    def conv(x_ref, p_ref, n_ref, w_ref):
        x = x_ref[...]
        w = w_ref[...]
        before = jnp.where(has_prev, p_ref[SUBLANES - 1:SUBLANES, :], 0.0)
        after = jnp.where(has_next, n_ref[0:1, :], 0.0)
        xm = jnp.where(rows == 0, before, pltpu.roll(x, 1, 0))
        xp = jnp.where(rows == tm - 1, after, pltpu.roll(x, tm - 1, 0))
        return xm * w[0:1] + x * w[1:2] + xp * w[2:3]

    r = conv(zr_ref, zrp_ref, zrn_ref, cwr_ref)
    k = conv(zk_ref, zkp_ref, zkn_ref, cwk_ref)
    v = conv(zv_ref, zvp_ref, zvn_ref, cwv_ref)
    ones_bd = _head_ones()

    kx = k * kk_ref[...]
    kkn = kx / jnp.maximum(jnp.sqrt(_head_sum(kx * kx, ones_bd)), 1e-12)
    r_o[...] = r
    v_o[...] = v
    kk_o[...] = kkn
    bonus_o[...] = _head_sum(r * k * rk_ref[...], ones_bd) * v

    zwa = zwa_ref[...]
    lw = jnp.tanh(zwa[:, :LORA_SLOT])
    la = zwa[:, LORA_SLOT:]
    ka = ka_ref[...]
    for d, (w2_ref, a2_ref) in enumerate(((w2a_ref, a2a_ref), (w2b_ref, a2b_ref))):
        xw = w0_ref[d:d + 1, :] + _dotp(lw, w2_ref[...], ta=2, tb=2)
        w_log = -jax.nn.softplus(-xw) - 0.5
        lw_o[d] = -jnp.exp(w_log)
        a = jax.nn.sigmoid(a0_ref[d:d + 1, :] + _dotp(la, a2_ref[...], ta=2, tb=2))
        kd_o[d] = k * (1.0 + (a - 1.0) * ka)
        kb_o[d] = kkn * a


def _rwkv_prep(z, lay, n_ctx, conv_w, w0, w2, a0, a2, k_k, k_a, r_k):
    bsz, t_all, _ = z.shape
    bw = lay["bw"]
    tm = 256
    cw = _pick(bw, (512, 256, 128))
    nblk = t_all // tm
    cb = n_ctx // tm
    sub = tm // SUBLANES
    last8 = t_all // SUBLANES - 1

    def main(off):
        return pl.BlockSpec((None, tm, cw), lambda b, j, c: (b, j, off // cw + c))

    def before(off):
        return pl.BlockSpec((None, SUBLANES, cw), lambda b, j, c: (b, jnp.maximum(j * sub - 1, 0), off // cw + c))

    def after(off):
        return pl.BlockSpec((None, SUBLANES, cw), lambda b, j, c: (b, jnp.minimum((j + 1) * sub, last8), off // cw + c))

    def chan(rows, off=0):
        return pl.BlockSpec((rows, cw), lambda b, j, c: (0, off // cw + c))

    def pad_lora(w):
        out = []
        for d in range(2):
            full = jnp.zeros((LORA_SLOT, bw), F32).at[d * LORA_RANK:(d + 1) * LORA_RANK].set(w[d])
            out.append(full)
        return out

    w2p = pad_lora(w2)
    a2p = pad_lora(a2)
    tok = pl.BlockSpec((None, tm, cw), lambda b, j, c: (b, j, c))
    tok2 = pl.BlockSpec((2, None, tm, cw), lambda b, j, c: (0, b, j, c))
    one = jax.ShapeDtypeStruct((bsz, t_all, bw), F32)
    two = jax.ShapeDtypeStruct((2, bsz, t_all, bw), F32)
    return pl.pallas_call(
        functools.partial(_prep_kernel, ctx_blocks=cb, n_blocks=nblk),
        grid=(bsz, nblk, bw // cw),
        in_specs=[
            main(lay["r"]), main(lay["kb"]), main(lay["vb"]),
            before(lay["r"]), before(lay["kb"]), before(lay["vb"]),
            after(lay["r"]), after(lay["kb"]), after(lay["vb"]),
            pl.BlockSpec((None, tm, 2 * LORA_SLOT), lambda b, j, c: (b, j, lay["lora"] // (2 * LORA_SLOT))),
            chan(3, 0), chan(3, bw), chan(3, 2 * bw),
            chan(LORA_SLOT), chan(LORA_SLOT), chan(LORA_SLOT), chan(LORA_SLOT),
            chan(2), chan(2), chan(1), chan(1), chan(1),
        ],
        out_specs=[tok, tok, tok, tok, tok2, tok2, tok2],
        out_shape=[one, one, one, one, two, two, two],
        compiler_params=_params("parallel", "parallel", "parallel"),
        name="rwkv_prep",
    )(z, z, z, z, z, z, z, z, z, z, conv_w, conv_w, conv_w, w2p[0], w2p[1], a2p[0], a2p[1],
      w0, a0, k_k.reshape(1, bw), k_a.reshape(1, bw), r_k.reshape(1, bw))


def _stack_heads(x, lane_is_first):
    return jnp.concatenate([jnp.where(lane_is_first, x, 0.0), jnp.where(lane_is_first, 0.0, x)], axis=0)


def _wkv_chunk(r, v, kk, lw, kd, kb, s_ref, y_ref, *, reverse, terms):
    ch = SCAN_CHUNK
    width = r.shape[1]
    dot = functools.partial(_dotp, ta=terms, tb=terms)

    row = lax.broadcasted_iota(jnp.int32, (ch, ch), 0)
    col = lax.broadcasted_iota(jnp.int32, (ch, ch), 1)
    incl = (row <= col) if reverse else (row >= col)
    cum = _dotp(incl.astype(F32), lw, ta=1, tb=3)
    last = cum[0:1, :] if reverse else cum[ch - 1:ch, :]
    g_in = jnp.exp(cum)
    g_inv = jnp.exp(-cum)
    g_ex = jnp.exp(cum - lw)
    g_rem = jnp.exp(last - cum)
    g_end = jnp.exp(last)
    a_t = -kk * g_ex
    b_t = kb * g_inv
    k_t = kd * g_inv
    r_t = r * g_in
    k_g = kd * g_rem
    b_g = kb * g_rem

    two = 2 * ch
    i2 = lax.broadcasted_iota(jnp.int32, (two, two), 0) % ch
    j2 = lax.broadcasted_iota(jnp.int32, (two, two), 1) % ch
    m_incl = (i2 <= j2) if reverse else (i2 >= j2)
    m_strict = (i2 < j2) if reverse else (i2 > j2)
    first = lax.broadcasted_iota(jnp.int32, (ch, LANES), 1) < B_HEAD_DIM

    for p in range(width // LANES):
        sl = slice(p * LANES, (p + 1) * LANES)
        st = functools.partial(_stack_heads, lane_is_first=first)
        a_s, r_s, b_s, k_s, v_s = st(a_t[:, sl]), st(r_t[:, sl]), st(b_t[:, sl]), st(k_t[:, sl]), st(v[:, sl])
        kg_s, bg_s = st(k_g[:, sl]), st(b_g[:, sl])
        lhs = jnp.concatenate([a_s, r_s], axis=0)
        dmat = dot(lhs, jnp.concatenate([b_s, k_s], axis=0), _NT)
        d_ab = jnp.where(m_strict, dmat[:two, :two], 0.0)
        d_ak = jnp.where(m_strict, dmat[:two, two:], 0.0)
        d_rb = jnp.where(m_incl, dmat[two:, :two], 0.0)
        d_rk = jnp.where(m_incl, dmat[two:, two:], 0.0)
        state = s_ref[p]
        from_state = dot(lhs, state, _NT)
        x = from_state[:two] + dot(d_ak, v_s)
        dk = d_ab
        for lvl in range(SCAN_LEVELS):
            if lvl + 1 < SCAN_LEVELS:
                res = dot(dk, jnp.concatenate([dk, x], axis=1))
                dk = res[:, :two]
                x = x + res[:, two:]
            else:
                x = x + dot(dk, x)
        u_s = x
        y_s = from_state[two:] + dot(jnp.concatenate([d_rk, d_rb], axis=1),
                                     jnp.concatenate([v_s, u_s], axis=0))
        y_ref[:, sl] = y_s[:ch] + y_s[ch:]
        s_ref[p] = state * g_end[:, sl] + dot(jnp.concatenate([v_s, u_s], axis=0),
                                              jnp.concatenate([kg_s, bg_s], axis=0), _TN)


def _wkv_kernel(rf_ref, vf_ref, kkf_ref, lwf_ref, kdf_ref, kbf_ref,
                rb_ref, vb_ref, kkb_ref, lwb_ref, kdb_ref, kbb_ref,
                yf_ref, yb_ref, sf_ref, sb_ref, *, terms):
    @pl.when(pl.program_id(2) == 0)
    def _():
        sf_ref[...] = jnp.zeros_like(sf_ref)
        sb_ref[...] = jnp.zeros_like(sb_ref)

    _wkv_chunk(rf_ref[...], vf_ref[...], kkf_ref[...], lwf_ref[...], kdf_ref[...], kbf_ref[...],
               sf_ref, yf_ref, reverse=False, terms=terms)
    _wkv_chunk(rb_ref[...], vb_ref[...], kkb_ref[...], lwb_ref[...], kdb_ref[...], kbb_ref[...],
               sb_ref, yb_ref, reverse=True, terms=terms)


def _wkv(r, v, kk, lw, kd, kb, n_ctx, terms):
    bsz, t_all, bw = r.shape
    ch = SCAN_CHUNK
    gw = _pick(bw, (SCAN_GROUP_WIDTH, 256, 128))
    nc = t_all // ch
    ncc = n_ctx // ch

    def fwd(c):
        return c

    def bwd(c):
        return jnp.where(c < ncc, ncc - 1 - c, nc + ncc - 1 - c)

    def tok(order):
        return pl.BlockSpec((None, ch, gw), lambda b, g, c: (b, order(c), g))

    def tok2(order, d):
        return pl.BlockSpec((None, None, ch, gw), lambda b, g, c: (d, b, order(c), g))

    out = jax.ShapeDtypeStruct((bsz, t_all, bw), F32)
    state = pltpu.VMEM((gw // LANES, LANES, LANES), F32)
    return pl.pallas_call(
        functools.partial(_wkv_kernel, terms=terms),
        grid=(bsz, bw // gw, nc),
        in_specs=[tok(fwd), tok(fwd), tok(fwd), tok2(fwd, 0), tok2(fwd, 0), tok2(fwd, 0),
                  tok(bwd), tok(bwd), tok(bwd), tok2(bwd, 1), tok2(bwd, 1), tok2(bwd, 1)],
        out_specs=[tok(fwd), tok(bwd)],
        out_shape=[out, out],
        scratch_shapes=[state, state],
        compiler_params=_params("parallel", "parallel", "arbitrary"),
        name="wkv_scan",
    )(r, v, kk, lw, kd, kb, r, v, kk, lw, kd, kb)


def _readout_kernel(yf_ref, yb_ref, bonus_ref, gate_ref, gw_ref, gb_ref, o_ref):
    ones_bd = _head_ones()
    y = yf_ref[...] + yb_ref[...]
    inv = 1.0 / B_HEAD_DIM
    mean = _head_sum(y, ones_bd) * inv
    yc = y - mean
    var = _head_sum(yc * yc, ones_bd) * inv
    out = yc * lax.rsqrt(var + GN_EPS) * gw_ref[...] + gb_ref[...]
    out = out + bonus_ref[...]
    o_ref[...] = (out * _silu(gate_ref[...])).astype(BF16)


def _readout(yf, yb, bonus, z, lay, n_ctx, gn_w, gn_b):
    bsz, t_all, bw = yf.shape
    n_lat = t_all - n_ctx
    tm = 256
    cw = _pick(bw, (512, 256, 128))
    off = n_ctx // tm
    tok = pl.BlockSpec((None, tm, cw), lambda b, j, c: (b, j + off, c))
    chan = pl.BlockSpec((1, cw), lambda b, j, c: (0, c))
    return pl.pallas_call(
        _readout_kernel,
        grid=(bsz, n_lat // tm, bw // cw),
        in_specs=[tok, tok, tok,
                  pl.BlockSpec((None, tm, cw), lambda b, j, c: (b, j + off, lay["gb"] // cw + c)),
                  chan, chan],
        out_specs=pl.BlockSpec((None, tm, cw), lambda b, j, c: (b, j, c)),
        out_shape=jax.ShapeDtypeStruct((bsz, n_lat, bw), BF16),
        compiler_params=_params("parallel", "parallel", "parallel"),
        name="rwkv_readout",
    )(yf, yb, bonus, z, gn_w.reshape(1, bw), gn_b.reshape(1, bw))


def _out_proj_kernel(a_ref, b_ref, wa_ref, wb_ref, x_ref, gate_ref, o_ref):
    y = jnp.dot(a_ref[...], wa_ref[...], preferred_element_type=F32)
    y = y + jnp.dot(b_ref[...], wb_ref[...], preferred_element_type=F32)
    o_ref[...] = x_ref[...] + gate_ref[...] * y


def _out_proj(mix_a, mix_b, w_a, w_b, x, mod):
    bsz, t, d = x.shape
    ka, kb = mix_a.shape[2], mix_b.shape[2]
    tm = _pick(t, (1024, 512, 256))
    tn = _pick(d, (512, 256, 128))
    mod3 = mod.reshape(mod.shape[0], 1, 3 * d)
    return pl.pallas_call(
        _out_proj_kernel,
        grid=(bsz, t // tm, d // tn),
        in_specs=[
            pl.BlockSpec((None, tm, ka), lambda b, i, j: (b, i, 0)),
            pl.BlockSpec((None, tm, kb), lambda b, i, j: (b, i, 0)),
            pl.BlockSpec((ka, tn), lambda b, i, j: (0, j)),
            pl.BlockSpec((kb, tn), lambda b, i, j: (0, j)),
            pl.BlockSpec((None, tm, tn), lambda b, i, j: (b, i, j)),
            pl.BlockSpec((None, 1, tn), lambda b, i, j: (b, 0, 2 * (d // tn) + j)),
        ],
        out_specs=pl.BlockSpec((None, tm, tn), lambda b, i, j: (b, i, j)),
        out_shape=jax.ShapeDtypeStruct((bsz, t, d), F32),
        compiler_params=_params("parallel", "parallel", "parallel"),
        name="out_proj",
    )(mix_a, mix_b, w_a, w_b, x, mod3)


def _fold_lanes(x):
    out = x[:, :LANES]
    for i in range(1, x.shape[1] // LANES):
        out = out + x[:, i * LANES:(i + 1) * LANES]
    return out


def _gm_in_kernel(h_ref, wu_ref, wv_ref, wg_ref, p_ref, gv_ref, s1_ref, s2_ref):
    j = pl.program_id(1)
    h = h_ref[...]
    u = jnp.dot(h, wu_ref[...], preferred_element_type=F32)
    g = jnp.dot(h, wg_ref[...], preferred_element_type=F32)
    p_ref[...] = (jax.nn.gelu(u) * _silu(g)).astype(BF16)
    gv = jax.nn.gelu(jnp.dot(h, wv_ref[...], preferred_element_type=F32))
    gv_ref[...] = gv

    @pl.when(j == 0)
    def _():
        s1_ref[...] = jnp.zeros_like(s1_ref)
        s2_ref[...] = jnp.zeros_like(s2_ref)

    s1_ref[...] += _fold_lanes(gv)
    s2_ref[...] += _fold_lanes(gv * gv)


def _gm_in(h, w_in, cwid):
    m, d = h.shape
    tm = _pick(m, (1024, 512, 256))
    tn = _pick(cwid, (256, 128))
    nt = cwid // tn
    stat = jax.ShapeDtypeStruct((m, LANES), F32)
    return pl.pallas_call(
        _gm_in_kernel,
        grid=(m // tm, nt),
        in_specs=[
            pl.BlockSpec((tm, d), lambda i, j: (i, 0)),
            pl.BlockSpec((d, tn), lambda i, j: (0, j)),
            pl.BlockSpec((d, tn), lambda i, j: (0, nt + j)),
            pl.BlockSpec((d, tn), lambda i, j: (0, 2 * nt + j)),
        ],
        out_specs=[
            pl.BlockSpec((tm, tn), lambda i, j: (i, j)),
            pl.BlockSpec((tm, tn), lambda i, j: (i, j)),
            pl.BlockSpec((tm, LANES), lambda i, j: (i, 0)),
            pl.BlockSpec((tm, LANES), lambda i, j: (i, 0)),
        ],
        out_shape=[jax.ShapeDtypeStruct((m, cwid), BF16), jax.ShapeDtypeStruct((m, cwid), F32), stat, stat],
        compiler_params=_params("parallel", "arbitrary"),
        name="gmlp_in",
    )(h, w_in, w_in, w_in)


def _gm_out_kernel(gv_ref, p_ref, s1_ref, s2_ref, lng_ref, lnb_ref, ws_ref, bs_ref, wo_ref, o_ref, *, cwid):
    g = pl.program_id(1)
    tm = gv_ref.shape[0]
    inv = 1.0 / cwid
    mean = jnp.sum(s1_ref[...], axis=-1, keepdims=True) * inv
    var = jnp.sum(s2_ref[...], axis=-1, keepdims=True) * inv - mean * mean
    rstd = lax.rsqrt(var + LN_EPS)
    vln = ((gv_ref[...] - mean) * rstd * lng_ref[...] + lnb_ref[...]).astype(BF16)
    ws = ws_ref[...].astype(BF16)
    bias = bs_ref[...]
    p = p_ref[...]
    ys = []
    for c in range(tm // C_CHUNK):
        rows = slice(c * C_CHUNK, (c + 1) * C_CHUNK)
        vm = jnp.dot(ws, vln[rows], preferred_element_type=F32) + bias
        ys.append((p[rows].astype(F32) * vm).astype(BF16))
    y = jnp.concatenate(ys, axis=0)
    contrib = jnp.dot(y, wo_ref[...], preferred_element_type=F32)

    @pl.when(g == 0)
    def _():
        o_ref[...] = contrib

    @pl.when(g > 0)
    def _():
        o_ref[...] += contrib


def _gm_out(gv, p, s1, s2, ln_g, ln_b, w_s, b_s, w_out):
    m, cwid = gv.shape
    groups = w_s.shape[0]
    gwid = cwid // groups
    d = w_out.shape[1]
    tm = _pick(m, (512, 256))
    return pl.pallas_call(
        functools.partial(_gm_out_kernel, cwid=cwid),
        grid=(m // tm, groups),
        in_specs=[
            pl.BlockSpec((tm, gwid), lambda i, g: (i, g)),
            pl.BlockSpec((tm, gwid), lambda i, g: (i, g)),
            pl.BlockSpec((tm, LANES), lambda i, g: (i, 0)),
            pl.BlockSpec((tm, LANES), lambda i, g: (i, 0)),
            pl.BlockSpec((1, gwid), lambda i, g: (0, g)),
            pl.BlockSpec((1, gwid), lambda i, g: (0, g)),
            pl.BlockSpec((None, C_CHUNK, C_CHUNK), lambda i, g: (g, 0, 0)),
            pl.BlockSpec((None, C_CHUNK, 1), lambda i, g: (g, 0, 0)),
            pl.BlockSpec((gwid, d), lambda i, g: (g, 0)),
        ],
        out_specs=pl.BlockSpec((tm, d), lambda i, g: (i, 0)),
        out_shape=jax.ShapeDtypeStruct((m, d), F32),
        compiler_params=_params("parallel", "arbitrary"),
        name="gmlp_out",
    )(gv, p, s1, s2, ln_g.reshape(1, cwid), ln_b.reshape(1, cwid), w_s,
      b_s.reshape(groups, C_CHUNK, 1), w_out)


def _final_kernel(x_ref, y_ref, gate_ref, g_ref, o_ref):
    x = x_ref[...] + gate_ref[...] * y_ref[...]
    ms = jnp.mean(x * x, axis=-1, keepdims=True)
    o_ref[...] = (x * lax.rsqrt(ms + NORM_EPS)) * g_ref[...]


def _final(x, y, mod, final_g):
    bsz, t, d = x.shape
    tm = 256
    mod3 = mod.reshape(mod.shape[0], 1, 3 * d)
    tok = pl.BlockSpec((None, tm, d), lambda b, j: (b, j, 0))
    return pl.pallas_call(
        _final_kernel,
        grid=(bsz, t // tm),
        in_specs=[tok, tok,
                  pl.BlockSpec((None, 1, d), lambda b, j: (b, 0, 2)),
                  pl.BlockSpec((1, d), lambda b, j: (0, 0))],
        out_specs=tok,
        out_shape=jax.ShapeDtypeStruct((bsz, t, d), F32),
        compiler_params=_params("parallel", "parallel"),
        name="final_norm",
    )(x, y.reshape(bsz, t, d), mod3, final_g.reshape(1, d))


def _ab_layout(d, heads, ab_in):
    aq = heads * A_HEAD_DIM
    bw = d // 2
    akv = (ab_in - 2 * aq - 4 * bw - 4 * LORA_RANK) // 2
    lay = {"aq": aq, "akv": akv, "bw": bw}
    off = 0
    for name, width in (("q", aq), ("ga", aq), ("r", bw), ("kb", bw), ("vb", bw), ("gb", bw),
                        ("lora", 2 * LORA_SLOT), ("k", akv), ("v", akv)):
        lay[name] = off
        off += width
    lay["width"] = off
    return lay


def _ab_weight(w_in, lay):
    aq, akv, bw = lay["aq"], lay["akv"], lay["bw"]
    widths = (aq, akv, akv, aq, bw, bw, bw, bw, 2 * LORA_RANK, 2 * LORA_RANK)
    parts = []
    off = 0
    for w in widths:
        parts.append(w_in[:, off:off + w])
        off += w
    q, k, v, ga, r, kb, vb, gb, lw, la = parts
    pad = jnp.zeros((w_in.shape[0], LORA_SLOT - 2 * LORA_RANK), w_in.dtype)
    return jnp.concatenate([q, ga, r, kb, vb, gb, lw, pad, la, pad, k, v], axis=1).astype(BF16)


def kernel(x, c, ctx, c_ctx, mod_w, mod_b, norm_g, ab_w_in, ab_w_out, attn_sink, rwkv_conv, rwkv_w0, rwkv_w2,
           rwkv_a0, rwkv_a2, rwkv_k_k, rwkv_k_a, rwkv_r_k, rwkv_gn_w, rwkv_gn_b, gm_w_in, gm_ln_g, gm_ln_b,
           gm_w_s, gm_b_s, gm_w_out, final_g):
    bsz, n_lat, d = x.shape
    n_ctx = ctx.shape[1]
    heads = attn_sink.shape[1]

    cvec = jnp.concatenate([c, c_ctx[None, :], jnp.zeros((SUBLANES - bsz - 1, d), F32)], axis=0)
    mod = _modulation(cvec, mod_w, mod_b)

    lay = _ab_layout(d, heads, ab_w_in.shape[2])
    w_in0 = _ab_weight(ab_w_in[0], lay)
    h0 = _norm_mod(x, ctx, norm_g[0], mod[0], bsz)
    t_all = n_ctx + n_lat
    z = _matmul(h0.reshape(bsz * t_all, d), w_in0).reshape(bsz, t_all, lay["width"])
    mix_a = _attention(z, attn_sink[0], _rope_tables(n_lat), lay, n_ctx, n_lat)
    r, v, kk, bonus, lw, kd, kb = _rwkv_prep(z, lay, n_ctx, rwkv_conv[0], rwkv_w0[0], rwkv_w2[0], rwkv_a0[0],
                                             rwkv_a2[0], rwkv_k_k[0], rwkv_k_a[0], rwkv_r_k[0])
    yf, yb = _wkv(r, v, kk, lw, kd, kb, n_ctx, terms=2)
    mix_b = _readout(yf, yb, bonus, z, lay, n_ctx, rwkv_gn_w[0], rwkv_gn_b[0])
    w_out0 = ab_w_out[0].astype(BF16)
    x1 = _out_proj(mix_a, mix_b, w_out0[:lay["aq"]], w_out0[lay["aq"]:], x, mod[0])

    cwid = gm_ln_g.shape[1]
    h1 = _norm_mod(x1, None, norm_g[1], mod[1], 0)
    p, gv, s1, s2 = _gm_in(h1.reshape(bsz * n_lat, d), gm_w_in[0].astype(BF16), cwid)
    y1 = _gm_out(gv, p, s1, s2, gm_ln_g[0], gm_ln_b[0], gm_w_s[0], gm_b_s[0], gm_w_out[0].astype(BF16))
    return _final(x1, y1, mod[1], final_g)
```

```python
import functools

import jax
import jax.numpy as jnp
from jax import lax
from jax.experimental import pallas as pl
from jax.experimental.pallas import tpu as pltpu

F32 = jnp.float32
BF16 = jnp.bfloat16

A_HEAD_DIM = 128
A_WINDOW = 128
A_BLOCK = 128
GRID_W = 64
ROPE_THETA = 10000.0
B_HEAD_DIM = 64
LORA_RANK = 96
GN_EPS = 64e-5
C_CHUNK = 128
LN_EPS = 1e-5
NORM_EPS = 1e-6

LANES = 128
SUBLANES = 8
VMEM_LIMIT_BYTES = 56 * 1024 * 1024

SCAN_CHUNK = 64
SCAN_LEVELS = 6
SCAN_GROUP_WIDTH = 512
LORA_SLOT = 256
MASK_VALUE = -1e30


def _params(*semantics):
    return pltpu.CompilerParams(dimension_semantics=semantics, vmem_limit_bytes=VMEM_LIMIT_BYTES)


def _pick(n, candidates):
    for c in candidates:
        if n % c == 0:
            return c
    raise ValueError(f"no tile in {candidates} divides {n}")


def _split(x, terms):
    parts = []
    rest = x
    for i in range(terms):
        p = rest.astype(BF16)
        parts.append(p)
        if i + 1 < terms:
            rest = rest - p.astype(F32)
    return parts


_NN = (((1,), (0,)), ((), ()))
_NT = (((1,), (1,)), ((), ()))
_TN = (((0,), (0,)), ((), ()))


def _dotp(a, b, dims=_NN, ta=1, tb=1):
    pa = _split(a, ta)
    pb = _split(b, tb)
    out = None
    for i in range(ta):
        for j in range(tb):
            if i + j >= max(ta, tb):
                continue
            t = lax.dot_general(pa[i], pb[j], dims, preferred_element_type=F32)
            out = t if out is None else out + t
    return out


def _silu(x):
    return x * jax.nn.sigmoid(x)


def _mod_kernel(c_ref, w_ref, b_ref, o_ref):
    s = _silu(c_ref[...])
    o_ref[...] = _dotp(s, w_ref[...], ta=2, tb=2) + b_ref[...]


def _modulation(cvec, mod_w, mod_b):
    depth, d, n = mod_w.shape
    rows = cvec.shape[0]
    tn = _pick(n, (512, 256, 128))
    return pl.pallas_call(
        _mod_kernel,
        grid=(depth, n // tn),
        in_specs=[
            pl.BlockSpec((rows, d), lambda l, j: (0, 0)),
            pl.BlockSpec((None, d, tn), lambda l, j: (l, 0, j)),
            pl.BlockSpec((None, 1, tn), lambda l, j: (l, 0, j)),
        ],
        out_specs=pl.BlockSpec((None, rows, tn), lambda l, j: (l, 0, j)),
        out_shape=jax.ShapeDtypeStruct((depth, rows, n), F32),
        compiler_params=_params("parallel", "parallel"),
        name="modulation",
    )(cvec, mod_w, mod_b.reshape(depth, 1, n))


def _norm_rows(x, g, shift, scale):
    ms = jnp.mean(x * x, axis=-1, keepdims=True)
    y = x * lax.rsqrt(ms + NORM_EPS)
    return (y * g) * (1.0 + scale) + shift


def _norm_mod_kernel(x_ref, c_ref, g_ref, shift_ref, scale_ref, o_ref, *, ctx_blocks):
    j = pl.program_id(1)

    @pl.when(j < ctx_blocks)
    def _():
        o_ref[...] = _norm_rows(c_ref[...], g_ref[...], shift_ref[...], scale_ref[...]).astype(BF16)

    @pl.when(j >= ctx_blocks)
    def _():
        o_ref[...] = _norm_rows(x_ref[...], g_ref[...], shift_ref[...], scale_ref[...]).astype(BF16)


def _norm_mod(x, ctx, g, mod, ctx_row):
    bsz, t, d = x.shape
    tm = 256
    if ctx is None:
        ctx = x
        cb = 0
    else:
        cb = ctx.shape[1] // tm
    nblk = cb + t // tm
    rows = mod.shape[0]
    mod3 = mod.reshape(rows, 1, 3 * d)

    def mrow(b, j):
        return jnp.where(j < cb, ctx_row, b)

    return pl.pallas_call(
        functools.partial(_norm_mod_kernel, ctx_blocks=cb),
        grid=(bsz, nblk),
        in_specs=[
            pl.BlockSpec((None, tm, d), lambda b, j: (b, jnp.maximum(j - cb, 0), 0)),
            pl.BlockSpec((None, tm, d), lambda b, j: (b, jnp.minimum(j, max(cb - 1, 0)), 0)),
            pl.BlockSpec((1, d), lambda b, j: (0, 0)),
            pl.BlockSpec((None, 1, d), lambda b, j: (mrow(b, j), 0, 0)),
            pl.BlockSpec((None, 1, d), lambda b, j: (mrow(b, j), 0, 1)),
        ],
        out_specs=pl.BlockSpec((None, tm, d), lambda b, j: (b, j, 0)),
        out_shape=jax.ShapeDtypeStruct((bsz, nblk * tm, d), BF16),
        compiler_params=_params("parallel", "parallel"),
        name="norm_mod",
    )(x, ctx, g.reshape(1, d), mod3, mod3)


def _mm_kernel(a_ref, b_ref, o_ref):
    o_ref[...] = jnp.dot(a_ref[...], b_ref[...], preferred_element_type=F32)


def _matmul(a, b):
    m, k = a.shape
    n = b.shape[1]
    tm = _pick(m, (768, 1024, 512, 256))
    tn = _pick(n, (512, 256, 128))
    return pl.pallas_call(
        _mm_kernel,
        grid=(m // tm, n // tn),
        in_specs=[
            pl.BlockSpec((tm, k), lambda i, j: (i, 0)),
            pl.BlockSpec((k, tn), lambda i, j: (0, j)),
        ],
        out_specs=pl.BlockSpec((tm, tn), lambda i, j: (i, j)),
        out_shape=jax.ShapeDtypeStruct((m, n), F32),
        compiler_params=_params("parallel", "parallel"),
        name="in_proj",
    )(a, b)


def _rope(x, cos, sin_signed):
    lane = lax.broadcasted_iota(jnp.int32, cos.shape, 1)
    low = (lane % 64) < 32
    out = []
    for h in range(x.shape[1] // A_HEAD_DIM):
        xh = x[:, h * A_HEAD_DIM:(h + 1) * A_HEAD_DIM]
        partner = jnp.where(low, pltpu.roll(xh, A_HEAD_DIM - 32, 1), pltpu.roll(xh, 32, 1))
        out.append(xh * cos + partner * sin_signed)
    return out


def _attn_kernel(sink_ref, q_ref, g_ref, kp_ref, kc_ref, kn_ref, vp_ref, vc_ref, vn_ref,
                 kx_ref, vx_ref, cq_ref, sq_ref, cp_ref, sp_ref, cn_ref, sn_ref, o_ref,
                 *, n_blocks, kv_heads, group):
    n = pl.program_id(1)
    blk = A_BLOCK
    lctx = kx_ref.shape[0]
    scale = A_HEAD_DIM ** -0.5

    q_heads = _rope(q_ref[...], cq_ref[...], sq_ref[...])
    kp = _rope(kp_ref[...], cp_ref[...], sp_ref[...])
    kc = _rope(kc_ref[...], cq_ref[...], sq_ref[...])
    kn = _rope(kn_ref[...], cn_ref[...], sn_ref[...])

    qi = lax.broadcasted_iota(jnp.int32, (blk, lctx + 3 * blk), 0)
    kj = lax.broadcasted_iota(jnp.int32, (blk, lctx + 3 * blk), 1) - lctx
    no_prev = jnp.where(n > 0, 0, blk)
    no_next = jnp.where(n < n_blocks - 1, 0, blk)
    prev_ok = (kj >= qi + no_prev) & (kj < blk)
    next_ok = (kj - 2 * blk <= qi - no_next) & (kj >= 2 * blk)
    valid = (kj < 0) | prev_ok | ((kj >= blk) & (kj < 2 * blk)) | next_ok
    mask = jnp.where(valid, 0.0, MASK_VALUE).astype(F32)
    mask = jnp.concatenate([mask] * group, axis=0)

    gate = g_ref[...]
    for hk in range(kv_heads):
        sl = slice(hk * A_HEAD_DIM, (hk + 1) * A_HEAD_DIM)
        kcat = jnp.concatenate([kx_ref[:, sl], kp[hk], kc[hk], kn[hk]], axis=0).astype(BF16)
        vcat = jnp.concatenate([vx_ref[:, sl], vp_ref[:, sl], vc_ref[:, sl], vn_ref[:, sl]], axis=0).astype(BF16)
        qs = jnp.concatenate([q_heads[hk * group + g] * scale for g in range(group)], axis=0).astype(BF16)
        s = lax.dot_general(qs, kcat, _NT, preferred_element_type=F32) + mask
        sink = jnp.concatenate(
            [jnp.full((blk, 1), sink_ref[hk * group + g], F32) for g in range(group)], axis=0)
        m = jnp.maximum(jnp.max(s, axis=-1, keepdims=True), sink)
        p = jnp.exp(s - m)
        den = jnp.exp(sink - m) + jnp.sum(p, axis=-1, keepdims=True)
        o = jnp.dot(p.astype(BF16), vcat, preferred_element_type=F32) / den
        for g in range(group):
            h = hk * group + g
            hs = slice(h * A_HEAD_DIM, (h + 1) * A_HEAD_DIM)
            o_ref[:, hs] = (o[g * blk:(g + 1) * blk] * _silu(gate[:, hs])).astype(BF16)


def _attention(z, sink, tables, lay, n_ctx, n_lat):
    bsz = z.shape[0]
    aq, akv = lay["aq"], lay["akv"]
    heads = aq // A_HEAD_DIM
    kv_heads = akv // A_HEAD_DIM
    blk = A_BLOCK
    nb = n_lat // blk
    off = n_ctx // blk
    kcol = lay["k"] // akv
    vcol = lay["v"] // akv
    cos, sin = tables

    def prev(n):
        return jnp.maximum(n - 1, 0)

    def nxt(n):
        return jnp.minimum(n + 1, nb - 1)

    def zspec(width, rowf, col):
        return pl.BlockSpec((None, blk, width), lambda b, n: (b, rowf(n) + off, col))

    def tspec(rowf):
        return pl.BlockSpec((blk, A_HEAD_DIM), lambda b, n: (rowf(n), 0))

    ident = lambda n: n
    return pl.pallas_call(
        functools.partial(_attn_kernel, n_blocks=nb, kv_heads=kv_heads, group=heads // kv_heads),
        grid=(bsz, nb),
        in_specs=[
            pl.BlockSpec(memory_space=pltpu.SMEM),
            zspec(aq, ident, lay["q"] // aq),
            zspec(aq, ident, lay["ga"] // aq),
            zspec(akv, prev, kcol), zspec(akv, ident, kcol), zspec(akv, nxt, kcol),
            zspec(akv, prev, vcol), zspec(akv, ident, vcol), zspec(akv, nxt, vcol),
            pl.BlockSpec((None, n_ctx, akv), lambda b, n: (b, 0, kcol)),
            pl.BlockSpec((None, n_ctx, akv), lambda b, n: (b, 0, vcol)),
            tspec(ident), tspec(ident), tspec(prev), tspec(prev), tspec(nxt), tspec(nxt),
        ],
        out_specs=pl.BlockSpec((None, blk, aq), lambda b, n: (b, n, 0)),
        out_shape=jax.ShapeDtypeStruct((bsz, n_lat, aq), BF16),
        compiler_params=_params("parallel", "parallel"),
        name="band_attention",
    )(sink, z, z, z, z, z, z, z, z, z, z, cos, sin, cos, sin, cos, sin)


def _rope_tables(n_tok):
    rows = n_tok // GRID_W
    row = jnp.repeat(jnp.arange(rows, dtype=F32), GRID_W)
    col = jnp.tile(jnp.arange(GRID_W, dtype=F32), rows)
    n_freq = A_HEAD_DIM // 4
    inv_freq = ROPE_THETA ** (-jnp.arange(n_freq, dtype=F32) / n_freq)
    ar = row[:, None] * inv_freq
    ac = col[:, None] * inv_freq
    cos = jnp.concatenate([jnp.cos(ar), jnp.cos(ar), jnp.cos(ac), jnp.cos(ac)], axis=-1)
    sin = jnp.concatenate([-jnp.sin(ar), jnp.sin(ar), -jnp.sin(ac), jnp.sin(ac)], axis=-1)
    return cos, sin


def _head_sum(x, ones_bd):
    out = []
    for i in range(x.shape[1] // LANES):
        out.append(_dotp(x[:, i * LANES:(i + 1) * LANES], ones_bd, ta=3, tb=1))
    return jnp.concatenate(out, axis=1) if len(out) > 1 else out[0]


def _head_ones():
    r = lax.broadcasted_iota(jnp.int32, (LANES, LANES), 0) // B_HEAD_DIM
    c = lax.broadcasted_iota(jnp.int32, (LANES, LANES), 1) // B_HEAD_DIM
    return (r == c).astype(F32)


def _prep_kernel(zr_ref, zk_ref, zv_ref, zrp_ref, zkp_ref, zvp_ref, zrn_ref, zkn_ref, zvn_ref,
                 zwa_ref, cwr_ref, cwk_ref, cwv_ref, w2a_ref, w2b_ref, a2a_ref, a2b_ref,
                 w0_ref, a0_ref, kk_ref, ka_ref, rk_ref,
                 r_o, v_o, kk_o, bonus_o, lw_o, kd_o, kb_o, *, ctx_blocks, n_blocks):
    j = pl.program_id(1)
    tm = zr_ref.shape[0]
    has_prev = jnp.logical_and(j != 0, j != ctx_blocks)
    has_next = jnp.logical_and(j != ctx_blocks - 1, j != n_blocks - 1)
    rows = lax.broadcasted_iota(jnp.int32, zr_ref.shape, 0)

    def conv(x_ref, p_ref, n_ref, w_ref):
        x = x_ref[...]
        w = w_ref[...]
        before = jnp.where(has_prev, p_ref[SUBLANES - 1:SUBLANES, :], 0.0)
        after = jnp.where(has_next, n_ref[0:1, :], 0.0)
        xm = jnp.where(rows == 0, before, pltpu.roll(x, 1, 0))
        xp = jnp.where(rows == tm - 1, after, pltpu.roll(x, tm - 1, 0))
        return xm * w[0:1] + x * w[1:2] + xp * w[2:3]

    r = conv(zr_ref, zrp_ref, zrn_ref, cwr_ref)
    k = conv(zk_ref, zkp_ref, zkn_ref, cwk_ref)
    v = conv(zv_ref, zvp_ref, zvn_ref, cwv_ref)
    ones_bd = _head_ones()

    kx = k * kk_ref[...]
    kkn = kx / jnp.maximum(jnp.sqrt(_head_sum(kx * kx, ones_bd)), 1e-12)
    r_o[...] = r
    v_o[...] = v
    kk_o[...] = kkn
    bonus_o[...] = _head_sum(r * k * rk_ref[...], ones_bd) * v

    zwa = zwa_ref[...]
    lw = jnp.tanh(zwa[:, :LORA_SLOT])
    la = zwa[:, LORA_SLOT:]
    ka = ka_ref[...]
    for d, (w2_ref, a2_ref) in enumerate(((w2a_ref, a2a_ref), (w2b_ref, a2b_ref))):
        xw = w0_ref[d:d + 1, :] + _dotp(lw, w2_ref[...], ta=2, tb=2)
        w_log = -jax.nn.softplus(-xw) - 0.5
        lw_o[d] = -jnp.exp(w_log)
        a = jax.nn.sigmoid(a0_ref[d:d + 1, :] + _dotp(la, a2_ref[...], ta=2, tb=2))
        kd_o[d] = k * (1.0 + (a - 1.0) * ka)
        kb_o[d] = kkn * a


def _rwkv_prep(z, lay, n_ctx, conv_w, w0, w2, a0, a2, k_k, k_a, r_k):
    bsz, t_all, _ = z.shape
    bw = lay["bw"]
    tm = 256
    cw = _pick(bw, (512, 256, 128))
    nblk = t_all // tm
    cb = n_ctx // tm
    sub = tm // SUBLANES
    last8 = t_all // SUBLANES - 1

    def main(off):
        return pl.BlockSpec((None, tm, cw), lambda b, j, c: (b, j, off // cw + c))

    def before(off):
        return pl.BlockSpec((None, SUBLANES, cw), lambda b, j, c: (b, jnp.maximum(j * sub - 1, 0), off // cw + c))

    def after(off):
        return pl.BlockSpec((None, SUBLANES, cw), lambda b, j, c: (b, jnp.minimum((j + 1) * sub, last8), off // cw + c))

    def chan(rows, off=0):
        return pl.BlockSpec((rows, cw), lambda b, j, c: (0, off // cw + c))

    def pad_lora(w):
        out = []
        for d in range(2):
            full = jnp.zeros((LORA_SLOT, bw), F32).at[d * LORA_RANK:(d + 1) * LORA_RANK].set(w[d])
            out.append(full)
        return out

    w2p = pad_lora(w2)
    a2p = pad_lora(a2)
    tok = pl.BlockSpec((None, tm, cw), lambda b, j, c: (b, j, c))
    tok2 = pl.BlockSpec((2, None, tm, cw), lambda b, j, c: (0, b, j, c))
    one = jax.ShapeDtypeStruct((bsz, t_all, bw), F32)
    two = jax.ShapeDtypeStruct((2, bsz, t_all, bw), F32)
    return pl.pallas_call(
        functools.partial(_prep_kernel, ctx_blocks=cb, n_blocks=nblk),
        grid=(bsz, nblk, bw // cw),
        in_specs=[
            main(lay["r"]), main(lay["kb"]), main(lay["vb"]),
            before(lay["r"]), before(lay["kb"]), before(lay["vb"]),
            after(lay["r"]), after(lay["kb"]), after(lay["vb"]),
            pl.BlockSpec((None, tm, 2 * LORA_SLOT), lambda b, j, c: (b, j, lay["lora"] // (2 * LORA_SLOT))),
            chan(3, 0), chan(3, bw), chan(3, 2 * bw),
            chan(LORA_SLOT), chan(LORA_SLOT), chan(LORA_SLOT), chan(LORA_SLOT),
            chan(2), chan(2), chan(1), chan(1), chan(1),
        ],
        out_specs=[tok, tok, tok, tok, tok2, tok2, tok2],
        out_shape=[one, one, one, one, two, two, two],
        compiler_params=_params("parallel", "parallel", "parallel"),
        name="rwkv_prep",
    )(z, z, z, z, z, z, z, z, z, z, conv_w, conv_w, conv_w, w2p[0], w2p[1], a2p[0], a2p[1],
      w0, a0, k_k.reshape(1, bw), k_a.reshape(1, bw), r_k.reshape(1, bw))


def _stack_heads(x, lane_is_first):
    return jnp.concatenate([jnp.where(lane_is_first, x, 0.0), jnp.where(lane_is_first, 0.0, x)], axis=0)


def _wkv_operands(r, v, kk, lw, kd, kb, reverse):
    ch = SCAN_CHUNK
    row = lax.broadcasted_iota(jnp.int32, (ch, ch), 0)
    col = lax.broadcasted_iota(jnp.int32, (ch, ch), 1)
    incl = (row <= col) if reverse else (row >= col)
    cum = _dotp(incl.astype(F32), lw, ta=1, tb=3)
    last = cum[0:1, :] if reverse else cum[ch - 1:ch, :]
    g_in = jnp.exp(cum)
    g_inv = jnp.exp(-cum)
    g_ex = jnp.exp(cum - lw)
    g_rem = jnp.exp(last - cum)
    g_end = jnp.exp(last)
    a_t = -kk * g_ex
    b_t = kb * g_inv
    k_t = kd * g_inv
    r_t = r * g_in
    k_g = kd * g_rem
    b_g = kb * g_rem
    return dict(a=a_t, r=r_t, b=b_t, k=k_t, v=v, kg=k_g, bg=b_g, g_end=g_end)


def _wkv_step(streams, terms):
    ch = SCAN_CHUNK
    two = 2 * ch
    dot = functools.partial(_dotp, ta=terms, tb=terms)
    i2 = lax.broadcasted_iota(jnp.int32, (two, two), 0) % ch
    j2 = lax.broadcasted_iota(jnp.int32, (two, two), 1) % ch
    first = lax.broadcasted_iota(jnp.int32, (ch, LANES), 1) < B_HEAD_DIM
    st = functools.partial(_stack_heads, lane_is_first=first)

    chains = []
    for ops, s_ref, y_ref, reverse in streams:
        m_incl = (i2 <= j2) if reverse else (i2 >= j2)
        m_strict = (i2 < j2) if reverse else (i2 > j2)
        for p in range(ops["a"].shape[1] // LANES):
            sl = slice(p * LANES, (p + 1) * LANES)
            c = {n: st(ops[n][:, sl]) for n in ("a", "r", "b", "k", "v", "kg", "bg")}
            c.update(sl=sl, p=p, s_ref=s_ref, y_ref=y_ref, m_incl=m_incl, m_strict=m_strict,
                     g_end=ops["g_end"][:, sl])
            chains.append(c)

    for c in chains:
        c["lhs"] = jnp.concatenate([c["a"], c["r"]], axis=0)
        c["dmat"] = dot(c["lhs"], jnp.concatenate([c["b"], c["k"]], axis=0), _NT)
    for c in chains:
        c["state"] = c["s_ref"][c["p"]]
        c["from_state"] = dot(c["lhs"], c["state"], _NT)
    for c in chains:
        d = c["dmat"]
        c["d_ab"] = jnp.where(c["m_strict"], d[:two, :two], 0.0)
        c["d_rb"] = jnp.where(c["m_incl"], d[two:, :two], 0.0)
        c["d_rk"] = jnp.where(c["m_incl"], d[two:, two:], 0.0)
        c["x"] = c["from_state"][:two] + dot(jnp.where(c["m_strict"], d[:two, two:], 0.0), c["v"])
    for lvl in range(SCAN_LEVELS):
        for c in chains:
            if lvl + 1 < SCAN_LEVELS:
                res = dot(c["d_ab"], jnp.concatenate([c["d_ab"], c["x"]], axis=1))
                c["d_ab"] = res[:, :two]
                c["x"] = c["x"] + res[:, two:]
            else:
                c["x"] = c["x"] + dot(c["d_ab"], c["x"])
    for c in chains:
        c["vu"] = jnp.concatenate([c["v"], c["x"]], axis=0)
        y_s = c["from_state"][two:] + dot(jnp.concatenate([c["d_rk"], c["d_rb"]], axis=1), c["vu"])
        c["y_ref"][:, c["sl"]] = y_s[:ch] + y_s[ch:]
    for c in chains:
        c["s_ref"][c["p"]] = c["state"] * c["g_end"] + dot(
            c["vu"], jnp.concatenate([c["kg"], c["bg"]], axis=0), _TN)


def _wkv_kernel(rf_ref, vf_ref, kkf_ref, lwf_ref, kdf_ref, kbf_ref,
                rb_ref, vb_ref, kkb_ref, lwb_ref, kdb_ref, kbb_ref,
                yf_ref, yb_ref, sf_ref, sb_ref, *, terms):
    @pl.when(pl.program_id(2) == 0)
    def _():
        sf_ref[...] = jnp.zeros_like(sf_ref)
        sb_ref[...] = jnp.zeros_like(sb_ref)

    fwd = _wkv_operands(rf_ref[...], vf_ref[...], kkf_ref[...], lwf_ref[...], kdf_ref[...], kbf_ref[...], False)
    bwd = _wkv_operands(rb_ref[...], vb_ref[...], kkb_ref[...], lwb_ref[...], kdb_ref[...], kbb_ref[...], True)
    _wkv_step([(fwd, sf_ref, yf_ref, False), (bwd, sb_ref, yb_ref, True)], terms)


def _wkv(r, v, kk, lw, kd, kb, n_ctx, terms):
    bsz, t_all, bw = r.shape
    ch = SCAN_CHUNK
    gw = _pick(bw, (SCAN_GROUP_WIDTH, 256, 128))
    nc = t_all // ch
    ncc = n_ctx // ch

    def fwd(c):
        return c

    def bwd(c):
        return jnp.where(c < ncc, ncc - 1 - c, nc + ncc - 1 - c)

    def tok(order):
        return pl.BlockSpec((None, ch, gw), lambda b, g, c: (b, order(c), g))

    def tok2(order, d):
        return pl.BlockSpec((None, None, ch, gw), lambda b, g, c: (d, b, order(c), g))

    out = jax.ShapeDtypeStruct((bsz, t_all, bw), F32)
    state = pltpu.VMEM((gw // LANES, LANES, LANES), F32)
    return pl.pallas_call(
        functools.partial(_wkv_kernel, terms=terms),
        grid=(bsz, bw // gw, nc),
        in_specs=[tok(fwd), tok(fwd), tok(fwd), tok2(fwd, 0), tok2(fwd, 0), tok2(fwd, 0),
                  tok(bwd), tok(bwd), tok(bwd), tok2(bwd, 1), tok2(bwd, 1), tok2(bwd, 1)],
        out_specs=[tok(fwd), tok(bwd)],
        out_shape=[out, out],
        scratch_shapes=[state, state],
        compiler_params=_params("parallel", "parallel", "arbitrary"),
        name="wkv_scan",
    )(r, v, kk, lw, kd, kb, r, v, kk, lw, kd, kb)


def _readout_kernel(yf_ref, yb_ref, bonus_ref, gate_ref, gw_ref, gb_ref, o_ref):
    ones_bd = _head_ones()
    y = yf_ref[...] + yb_ref[...]
    inv = 1.0 / B_HEAD_DIM
    mean = _head_sum(y, ones_bd) * inv
    yc = y - mean
    var = _head_sum(yc * yc, ones_bd) * inv
    out = yc * lax.rsqrt(var + GN_EPS) * gw_ref[...] + gb_ref[...]
    out = out + bonus_ref[...]
    o_ref[...] = (out * _silu(gate_ref[...])).astype(BF16)


def _readout(yf, yb, bonus, z, lay, n_ctx, gn_w, gn_b):
    bsz, t_all, bw = yf.shape
    n_lat = t_all - n_ctx
    tm = 256
    cw = _pick(bw, (512, 256, 128))
    off = n_ctx // tm
    tok = pl.BlockSpec((None, tm, cw), lambda b, j, c: (b, j + off, c))
    chan = pl.BlockSpec((1, cw), lambda b, j, c: (0, c))
    return pl.pallas_call(
        _readout_kernel,
        grid=(bsz, n_lat // tm, bw // cw),
        in_specs=[tok, tok, tok,
                  pl.BlockSpec((None, tm, cw), lambda b, j, c: (b, j + off, lay["gb"] // cw + c)),
                  chan, chan],
        out_specs=pl.BlockSpec((None, tm, cw), lambda b, j, c: (b, j, c)),
        out_shape=jax.ShapeDtypeStruct((bsz, n_lat, bw), BF16),
        compiler_params=_params("parallel", "parallel", "parallel"),
        name="rwkv_readout",
    )(yf, yb, bonus, z, gn_w.reshape(1, bw), gn_b.reshape(1, bw))


def _out_proj_kernel(a_ref, b_ref, wa_ref, wb_ref, x_ref, gate_ref, o_ref):
    y = jnp.dot(a_ref[...], wa_ref[...], preferred_element_type=F32)
    y = y + jnp.dot(b_ref[...], wb_ref[...], preferred_element_type=F32)
    o_ref[...] = x_ref[...] + gate_ref[...] * y


def _out_proj(mix_a, mix_b, w_a, w_b, x, mod):
    bsz, t, d = x.shape
    ka, kb = mix_a.shape[2], mix_b.shape[2]
    tm = _pick(t, (1024, 512, 256))
    tn = _pick(d, (512, 256, 128))
    mod3 = mod.reshape(mod.shape[0], 1, 3 * d)
    return pl.pallas_call(
        _out_proj_kernel,
        grid=(bsz, t // tm, d // tn),
        in_specs=[
            pl.BlockSpec((None, tm, ka), lambda b, i, j: (b, i, 0)),
            pl.BlockSpec((None, tm, kb), lambda b, i, j: (b, i, 0)),
            pl.BlockSpec((ka, tn), lambda b, i, j: (0, j)),
            pl.BlockSpec((kb, tn), lambda b, i, j: (0, j)),
            pl.BlockSpec((None, tm, tn), lambda b, i, j: (b, i, j)),
            pl.BlockSpec((None, 1, tn), lambda b, i, j: (b, 0, 2 * (d // tn) + j)),
        ],
        out_specs=pl.BlockSpec((None, tm, tn), lambda b, i, j: (b, i, j)),
        out_shape=jax.ShapeDtypeStruct((bsz, t, d), F32),
        compiler_params=_params("parallel", "parallel", "parallel"),
        name="out_proj",
    )(mix_a, mix_b, w_a, w_b, x, mod3)


def _fold_lanes(x):
    out = x[:, :LANES]
    for i in range(1, x.shape[1] // LANES):
        out = out + x[:, i * LANES:(i + 1) * LANES]
    return out


def _gm_in_kernel(h_ref, wu_ref, wv_ref, wg_ref, p_ref, gv_ref, s1_ref, s2_ref):
    j = pl.program_id(1)
    h = h_ref[...]
    u = jnp.dot(h, wu_ref[...], preferred_element_type=F32)
    g = jnp.dot(h, wg_ref[...], preferred_element_type=F32)
    p_ref[...] = (jax.nn.gelu(u) * _silu(g)).astype(BF16)
    gv = jax.nn.gelu(jnp.dot(h, wv_ref[...], preferred_element_type=F32))
    gv_ref[...] = gv

    @pl.when(j == 0)
    def _():
        s1_ref[...] = jnp.zeros_like(s1_ref)
        s2_ref[...] = jnp.zeros_like(s2_ref)

    s1_ref[...] += _fold_lanes(gv)
    s2_ref[...] += _fold_lanes(gv * gv)


def _gm_in(h, w_in, cwid):
    m, d = h.shape
    tm = _pick(m, (1024, 512, 256))
    tn = _pick(cwid, (256, 128))
    nt = cwid // tn
    stat = jax.ShapeDtypeStruct((m, LANES), F32)
    return pl.pallas_call(
        _gm_in_kernel,
        grid=(m // tm, nt),
        in_specs=[
            pl.BlockSpec((tm, d), lambda i, j: (i, 0)),
            pl.BlockSpec((d, tn), lambda i, j: (0, j)),
            pl.BlockSpec((d, tn), lambda i, j: (0, nt + j)),
            pl.BlockSpec((d, tn), lambda i, j: (0, 2 * nt + j)),
        ],
        out_specs=[
            pl.BlockSpec((tm, tn), lambda i, j: (i, j)),
            pl.BlockSpec((tm, tn), lambda i, j: (i, j)),
            pl.BlockSpec((tm, LANES), lambda i, j: (i, 0)),
            pl.BlockSpec((tm, LANES), lambda i, j: (i, 0)),
        ],
        out_shape=[jax.ShapeDtypeStruct((m, cwid), BF16), jax.ShapeDtypeStruct((m, cwid), F32), stat, stat],
        compiler_params=_params("parallel", "arbitrary"),
        name="gmlp_in",
    )(h, w_in, w_in, w_in)


def _gm_out_kernel(gv_ref, p_ref, s1_ref, s2_ref, lng_ref, lnb_ref, ws_ref, bs_ref, wo_ref, o_ref, *, cwid):
    g = pl.program_id(1)
    tm = gv_ref.shape[0]
    inv = 1.0 / cwid
    mean = jnp.sum(s1_ref[...], axis=-1, keepdims=True) * inv
    var = jnp.sum(s2_ref[...], axis=-1, keepdims=True) * inv - mean * mean
    rstd = lax.rsqrt(var + LN_EPS)
    vln = ((gv_ref[...] - mean) * rstd * lng_ref[...] + lnb_ref[...]).astype(BF16)
    ws = ws_ref[...].astype(BF16)
    bias = bs_ref[...]
    p = p_ref[...]
    ys = []
    for c in range(tm // C_CHUNK):
        rows = slice(c * C_CHUNK, (c + 1) * C_CHUNK)
        vm = jnp.dot(ws, vln[rows], preferred_element_type=F32) + bias
        ys.append((p[rows].astype(F32) * vm).astype(BF16))
    y = jnp.concatenate(ys, axis=0)
    contrib = jnp.dot(y, wo_ref[...], preferred_element_type=F32)

    @pl.when(g == 0)
    def _():
        o_ref[...] = contrib

    @pl.when(g > 0)
    def _():
        o_ref[...] += contrib


def _gm_out(gv, p, s1, s2, ln_g, ln_b, w_s, b_s, w_out):
    m, cwid = gv.shape
    groups = w_s.shape[0]
    gwid = cwid // groups
    d = w_out.shape[1]
    tm = _pick(m, (512, 256))
    return pl.pallas_call(
        functools.partial(_gm_out_kernel, cwid=cwid),
        grid=(m // tm, groups),
        in_specs=[
            pl.BlockSpec((tm, gwid), lambda i, g: (i, g)),
            pl.BlockSpec((tm, gwid), lambda i, g: (i, g)),
            pl.BlockSpec((tm, LANES), lambda i, g: (i, 0)),
            pl.BlockSpec((tm, LANES), lambda i, g: (i, 0)),
            pl.BlockSpec((1, gwid), lambda i, g: (0, g)),
            pl.BlockSpec((1, gwid), lambda i, g: (0, g)),
            pl.BlockSpec((None, C_CHUNK, C_CHUNK), lambda i, g: (g, 0, 0)),
            pl.BlockSpec((None, C_CHUNK, 1), lambda i, g: (g, 0, 0)),
            pl.BlockSpec((gwid, d), lambda i, g: (g, 0)),
        ],
        out_specs=pl.BlockSpec((tm, d), lambda i, g: (i, 0)),
        out_shape=jax.ShapeDtypeStruct((m, d), F32),
        compiler_params=_params("parallel", "arbitrary"),
        name="gmlp_out",
    )(gv, p, s1, s2, ln_g.reshape(1, cwid), ln_b.reshape(1, cwid), w_s,
      b_s.reshape(groups, C_CHUNK, 1), w_out)


def _final_kernel(x_ref, y_ref, gate_ref, g_ref, o_ref):
    x = x_ref[...] + gate_ref[...] * y_ref[...]
    ms = jnp.mean(x * x, axis=-1, keepdims=True)
    o_ref[...] = (x * lax.rsqrt(ms + NORM_EPS)) * g_ref[...]


def _final(x, y, mod, final_g):
    bsz, t, d = x.shape
    tm = 256
    mod3 = mod.reshape(mod.shape[0], 1, 3 * d)
    tok = pl.BlockSpec((None, tm, d), lambda b, j: (b, j, 0))
    return pl.pallas_call(
        _final_kernel,
        grid=(bsz, t // tm),
        in_specs=[tok, tok,
                  pl.BlockSpec((None, 1, d), lambda b, j: (b, 0, 2)),
                  pl.BlockSpec((1, d), lambda b, j: (0, 0))],
        out_specs=tok,
        out_shape=jax.ShapeDtypeStruct((bsz, t, d), F32),
        compiler_params=_params("parallel", "parallel"),
        name="final_norm",
    )(x, y.reshape(bsz, t, d), mod3, final_g.reshape(1, d))


def _ab_layout(d, heads, ab_in):
    aq = heads * A_HEAD_DIM
    bw = d // 2
    akv = (ab_in - 2 * aq - 4 * bw - 4 * LORA_RANK) // 2
    lay = {"aq": aq, "akv": akv, "bw": bw}
    off = 0
    for name, width in (("q", aq), ("ga", aq), ("r", bw), ("kb", bw), ("vb", bw), ("gb", bw),
                        ("lora", 2 * LORA_SLOT), ("k", akv), ("v", akv)):
        lay[name] = off
        off += width
    lay["width"] = off
    return lay


def _ab_weight(w_in, lay):
    aq, akv, bw = lay["aq"], lay["akv"], lay["bw"]
    widths = (aq, akv, akv, aq, bw, bw, bw, bw, 2 * LORA_RANK, 2 * LORA_RANK)
    parts = []
    off = 0
    for w in widths:
        parts.append(w_in[:, off:off + w])
        off += w
    q, k, v, ga, r, kb, vb, gb, lw, la = parts
    pad = jnp.zeros((w_in.shape[0], LORA_SLOT - 2 * LORA_RANK), w_in.dtype)
    return jnp.concatenate([q, ga, r, kb, vb, gb, lw, pad, la, pad, k, v], axis=1).astype(BF16)


def kernel(x, c, ctx, c_ctx, mod_w, mod_b, norm_g, ab_w_in, ab_w_out, attn_sink, rwkv_conv, rwkv_w0, rwkv_w2,
           rwkv_a0, rwkv_a2, rwkv_k_k, rwkv_k_a, rwkv_r_k, rwkv_gn_w, rwkv_gn_b, gm_w_in, gm_ln_g, gm_ln_b,
           gm_w_s, gm_b_s, gm_w_out, final_g):
    bsz, n_lat, d = x.shape
    n_ctx = ctx.shape[1]
    heads = attn_sink.shape[1]

    cvec = jnp.concatenate([c, c_ctx[None, :], jnp.zeros((SUBLANES - bsz - 1, d), F32)], axis=0)
    mod = _modulation(cvec, mod_w, mod_b)

    lay = _ab_layout(d, heads, ab_w_in.shape[2])
    w_in0 = _ab_weight(ab_w_in[0], lay)
    h0 = _norm_mod(x, ctx, norm_g[0], mod[0], bsz)
    t_all = n_ctx + n_lat
    z = _matmul(h0.reshape(bsz * t_all, d), w_in0).reshape(bsz, t_all, lay["width"])
    mix_a = _attention(z, attn_sink[0], _rope_tables(n_lat), lay, n_ctx, n_lat)
    r, v, kk, bonus, lw, kd, kb = _rwkv_prep(z, lay, n_ctx, rwkv_conv[0], rwkv_w0[0], rwkv_w2[0], rwkv_a0[0],
                                             rwkv_a2[0], rwkv_k_k[0], rwkv_k_a[0], rwkv_r_k[0])
    yf, yb = _wkv(r, v, kk, lw, kd, kb, n_ctx, terms=1)
    mix_b = _readout(yf, yb, bonus, z, lay, n_ctx, rwkv_gn_w[0], rwkv_gn_b[0])
    w_out0 = ab_w_out[0].astype(BF16)
    x1 = _out_proj(mix_a, mix_b, w_out0[:lay["aq"]], w_out0[lay["aq"]:], x, mod[0])

    cwid = gm_ln_g.shape[1]
    h1 = _norm_mod(x1, None, norm_g[1], mod[1], 0)
    p, gv, s1, s2 = _gm_in(h1.reshape(bsz * n_lat, d), gm_w_in[0].astype(BF16), cwid)
    y1 = _gm_out(gv, p, s1, s2, gm_ln_g[0], gm_ln_b[0], gm_w_s[0], gm_b_s[0], gm_w_out[0].astype(BF16))
    return _final(x1, y1, mod[1], final_g)
```

```python
import functools
import math

import jax
import jax.numpy as jnp
from jax import lax
from jax.experimental import pallas as pl
from jax.experimental.pallas import tpu as pltpu

F32 = jnp.float32
BF16 = jnp.bfloat16

A_HEAD_DIM = 128
A_WINDOW = 128
A_BLOCK = 128
GRID_W = 64
ROPE_THETA = 10000.0
B_HEAD_DIM = 64
LORA_RANK = 96
GN_EPS = 64e-5
C_CHUNK = 128
LN_EPS = 1e-5
NORM_EPS = 1e-6

LANES = 128
SUBLANES = 8
VMEM_LIMIT_BYTES = 56 * 1024 * 1024

SCAN_CHUNK = 64
SCAN_LEVELS = 6
SCAN_GROUP_WIDTH = 512
LORA_SLOT = 256
MASK_VALUE = -1e30
DECAY_SCALE = math.exp(-0.5)


def _params(*semantics):
    return pltpu.CompilerParams(dimension_semantics=semantics, vmem_limit_bytes=VMEM_LIMIT_BYTES)


def _pick(n, candidates):
    for c in candidates:
        if n % c == 0:
            return c
    raise ValueError(f"no tile in {candidates} divides {n}")


def _split(x, terms):
    parts = []
    rest = x
    for i in range(terms):
        p = rest.astype(BF16)
        parts.append(p)
        if i + 1 < terms:
            rest = rest - p.astype(F32)
    return parts


_NN = (((1,), (0,)), ((), ()))
_NT = (((1,), (1,)), ((), ()))
_TN = (((0,), (0,)), ((), ()))


def _dotp(a, b, dims=_NN, ta=1, tb=1):
    pa = _split(a, ta)
    pb = _split(b, tb)
    out = None
    for i in range(ta):
        for j in range(tb):
            if i + j >= max(ta, tb):
                continue
            t = lax.dot_general(pa[i], pb[j], dims, preferred_element_type=F32)
            out = t if out is None else out + t
    return out


def _silu(x):
    return x * jax.nn.sigmoid(x)


def _mod_kernel(c_ref, w_ref, b_ref, o_ref):
    s = _silu(c_ref[...])
    o_ref[...] = _dotp(s, w_ref[...], ta=2, tb=2) + b_ref[...]


def _modulation(cvec, mod_w, mod_b):
    depth, d, n = mod_w.shape
    rows = cvec.shape[0]
    tn = _pick(n, (512, 256, 128))
    return pl.pallas_call(
        _mod_kernel,
        grid=(depth, n // tn),
        in_specs=[
            pl.BlockSpec((rows, d), lambda l, j: (0, 0)),
            pl.BlockSpec((None, d, tn), lambda l, j: (l, 0, j)),
            pl.BlockSpec((None, 1, tn), lambda l, j: (l, 0, j)),
        ],
        out_specs=pl.BlockSpec((None, rows, tn), lambda l, j: (l, 0, j)),
        out_shape=jax.ShapeDtypeStruct((depth, rows, n), F32),
        compiler_params=_params("parallel", "parallel"),
        name="modulation",
    )(cvec, mod_w, mod_b.reshape(depth, 1, n))


def _norm_rows(x, g, shift, scale):
    ms = jnp.mean(x * x, axis=-1, keepdims=True)
    y = x * lax.rsqrt(ms + NORM_EPS)
    return (y * g) * (1.0 + scale) + shift


def _norm_mod_kernel(x_ref, c_ref, g_ref, shift_ref, scale_ref, o_ref, *, ctx_blocks):
    j = pl.program_id(1)

    @pl.when(j < ctx_blocks)
    def _():
        o_ref[...] = _norm_rows(c_ref[...], g_ref[...], shift_ref[...], scale_ref[...]).astype(BF16)

    @pl.when(j >= ctx_blocks)
    def _():
        o_ref[...] = _norm_rows(x_ref[...], g_ref[...], shift_ref[...], scale_ref[...]).astype(BF16)


def _norm_mod(x, ctx, g, mod, ctx_row):
    bsz, t, d = x.shape
    tm = 256
    if ctx is None:
        ctx = x
        cb = 0
    else:
        cb = ctx.shape[1] // tm
    nblk = cb + t // tm
    rows = mod.shape[0]
    mod3 = mod.reshape(rows, 1, 3 * d)

    def mrow(b, j):
        return jnp.where(j < cb, ctx_row, b)

    return pl.pallas_call(
        functools.partial(_norm_mod_kernel, ctx_blocks=cb),
        grid=(bsz, nblk),
        in_specs=[
            pl.BlockSpec((None, tm, d), lambda b, j: (b, jnp.maximum(j - cb, 0), 0)),
            pl.BlockSpec((None, tm, d), lambda b, j: (b, jnp.minimum(j, max(cb - 1, 0)), 0)),
            pl.BlockSpec((1, d), lambda b, j: (0, 0)),
            pl.BlockSpec((None, 1, d), lambda b, j: (mrow(b, j), 0, 0)),
            pl.BlockSpec((None, 1, d), lambda b, j: (mrow(b, j), 0, 1)),
        ],
        out_specs=pl.BlockSpec((None, tm, d), lambda b, j: (b, j, 0)),
        out_shape=jax.ShapeDtypeStruct((bsz, nblk * tm, d), BF16),
        compiler_params=_params("parallel", "parallel"),
        name="norm_mod",
    )(x, ctx, g.reshape(1, d), mod3, mod3)


def _mm_kernel(a_ref, b_ref, o_ref):
    o_ref[...] = jnp.dot(a_ref[...], b_ref[...], preferred_element_type=F32)


def _matmul(a, b):
    m, k = a.shape
    n = b.shape[1]
    tm = _pick(m, (768, 1024, 512, 256))
    tn = _pick(n, (512, 256, 128))
    return pl.pallas_call(
        _mm_kernel,
        grid=(m // tm, n // tn),
        in_specs=[
            pl.BlockSpec((tm, k), lambda i, j: (i, 0)),
            pl.BlockSpec((k, tn), lambda i, j: (0, j)),
        ],
        out_specs=pl.BlockSpec((tm, tn), lambda i, j: (i, j)),
        out_shape=jax.ShapeDtypeStruct((m, n), F32),
        compiler_params=_params("parallel", "parallel"),
        name="in_proj",
    )(a, b)


def _rope(x, cos, sin_signed):
    out = []
    for h in range(x.shape[1] // A_HEAD_DIM):
        xh = x[:, h * A_HEAD_DIM:(h + 1) * A_HEAD_DIM]
        out.append(xh * cos + pltpu.roll(xh, A_HEAD_DIM // 2, 1) * sin_signed)
    return out


def _rope_head_order(w, n_heads):
    w5 = w.reshape(w.shape[0], n_heads, 2, 2, A_HEAD_DIM // 4)
    return jnp.swapaxes(w5, 2, 3).reshape(w.shape)


def _attn_kernel(sink_ref, q_ref, g_ref, kp_ref, kc_ref, kn_ref, vp_ref, vc_ref, vn_ref,
                 kx_ref, vx_ref, cq_ref, sq_ref, cp_ref, sp_ref, cn_ref, sn_ref, o_ref,
                 *, n_blocks, kv_heads, group):
    n = pl.program_id(1)
    blk = A_BLOCK
    lctx = kx_ref.shape[0]
    scale = A_HEAD_DIM ** -0.5

    q_heads = _rope(q_ref[...], cq_ref[...], sq_ref[...])
    kp = _rope(kp_ref[...], cp_ref[...], sp_ref[...])
    kc = _rope(kc_ref[...], cq_ref[...], sq_ref[...])
    kn = _rope(kn_ref[...], cn_ref[...], sn_ref[...])

    qi = lax.broadcasted_iota(jnp.int32, (blk, lctx + 3 * blk), 0)
    kj = lax.broadcasted_iota(jnp.int32, (blk, lctx + 3 * blk), 1) - lctx
    no_prev = jnp.where(n > 0, 0, blk)
    no_next = jnp.where(n < n_blocks - 1, 0, blk)
    prev_ok = (kj >= qi + no_prev) & (kj < blk)
    next_ok = (kj - 2 * blk <= qi - no_next) & (kj >= 2 * blk)
    valid = (kj < 0) | prev_ok | ((kj >= blk) & (kj < 2 * blk)) | next_ok
    mask = jnp.where(valid, 0.0, MASK_VALUE).astype(F32)
    mask = jnp.concatenate([mask] * group, axis=0)

    gate = g_ref[...]
    for hk in range(kv_heads):
        sl = slice(hk * A_HEAD_DIM, (hk + 1) * A_HEAD_DIM)
        kcat = jnp.concatenate([kx_ref[:, sl], kp[hk], kc[hk], kn[hk]], axis=0).astype(BF16)
        vcat = jnp.concatenate([vx_ref[:, sl], vp_ref[:, sl], vc_ref[:, sl], vn_ref[:, sl]], axis=0).astype(BF16)
        qs = jnp.concatenate([q_heads[hk * group + g] * scale for g in range(group)], axis=0).astype(BF16)
        s = lax.dot_general(qs, kcat, _NT, preferred_element_type=F32) + mask
        sink = jnp.concatenate(
            [jnp.full((blk, 1), sink_ref[hk * group + g], F32) for g in range(group)], axis=0)
        m = jnp.maximum(jnp.max(s, axis=-1, keepdims=True), sink)
        p = jnp.exp(s - m)
        den = jnp.exp(sink - m) + jnp.sum(p, axis=-1, keepdims=True)
        o = jnp.dot(p.astype(BF16), vcat, preferred_element_type=F32) / den
        for g in range(group):
            h = hk * group + g
            hs = slice(h * A_HEAD_DIM, (h + 1) * A_HEAD_DIM)
            o_ref[:, hs] = (o[g * blk:(g + 1) * blk] * _silu(gate[:, hs])).astype(BF16)


def _attention(z, sink, tables, lay, n_ctx, n_lat):
    bsz = z.shape[0]
    aq, akv = lay["aq"], lay["akv"]
    heads = aq // A_HEAD_DIM
    kv_heads = akv // A_HEAD_DIM
    blk = A_BLOCK
    nb = n_lat // blk
    off = n_ctx // blk
    kcol = lay["k"] // akv
    vcol = lay["v"] // akv
    cos, sin = tables

    def prev(n):
        return jnp.maximum(n - 1, 0)

    def nxt(n):
        return jnp.minimum(n + 1, nb - 1)

    def zspec(width, rowf, col):
        return pl.BlockSpec((None, blk, width), lambda b, n: (b, rowf(n) + off, col))

    def tspec(rowf):
        return pl.BlockSpec((blk, A_HEAD_DIM), lambda b, n: (rowf(n), 0))

    ident = lambda n: n
    return pl.pallas_call(
        functools.partial(_attn_kernel, n_blocks=nb, kv_heads=kv_heads, group=heads // kv_heads),
        grid=(bsz, nb),
        in_specs=[
            pl.BlockSpec(memory_space=pltpu.SMEM),
            zspec(aq, ident, lay["q"] // aq),
            zspec(aq, ident, lay["ga"] // aq),
            zspec(akv, prev, kcol), zspec(akv, ident, kcol), zspec(akv, nxt, kcol),
            zspec(akv, prev, vcol), zspec(akv, ident, vcol), zspec(akv, nxt, vcol),
            pl.BlockSpec((None, n_ctx, akv), lambda b, n: (b, 0, kcol)),
            pl.BlockSpec((None, n_ctx, akv), lambda b, n: (b, 0, vcol)),
            tspec(ident), tspec(ident), tspec(prev), tspec(prev), tspec(nxt), tspec(nxt),
        ],
        out_specs=pl.BlockSpec((None, blk, aq), lambda b, n: (b, n, 0)),
        out_shape=jax.ShapeDtypeStruct((bsz, n_lat, aq), BF16),
        compiler_params=_params("parallel", "parallel"),
        name="band_attention",
    )(sink, z, z, z, z, z, z, z, z, z, z, cos, sin, cos, sin, cos, sin)


def _rope_tables(n_tok):
    rows = n_tok // GRID_W
    row = jnp.repeat(jnp.arange(rows, dtype=F32), GRID_W)
    col = jnp.tile(jnp.arange(GRID_W, dtype=F32), rows)
    n_freq = A_HEAD_DIM // 4
    inv_freq = ROPE_THETA ** (-jnp.arange(n_freq, dtype=F32) / n_freq)
    ar = row[:, None] * inv_freq
    ac = col[:, None] * inv_freq
    cos = jnp.concatenate([jnp.cos(ar), jnp.cos(ac), jnp.cos(ar), jnp.cos(ac)], axis=-1)
    sin = jnp.concatenate([-jnp.sin(ar), -jnp.sin(ac), jnp.sin(ar), jnp.sin(ac)], axis=-1)
    return cos, sin


def _head_sum(x, ones_bd):
    out = []
    for i in range(x.shape[1] // LANES):
        out.append(_dotp(x[:, i * LANES:(i + 1) * LANES], ones_bd, ta=2, tb=1))
    return jnp.concatenate(out, axis=1) if len(out) > 1 else out[0]


def _head_ones():
    r = lax.broadcasted_iota(jnp.int32, (LANES, LANES), 0) // B_HEAD_DIM
    c = lax.broadcasted_iota(jnp.int32, (LANES, LANES), 1) // B_HEAD_DIM
    return (r == c).astype(F32)


def _prep_kernel(zr_ref, zk_ref, zv_ref, zrp_ref, zkp_ref, zvp_ref, zrn_ref, zkn_ref, zvn_ref,
                 zwa_ref, cwr_ref, cwk_ref, cwv_ref, w2_ref, a2_ref,
                 w0_ref, a0_ref, kk_ref, ka_ref, rk_ref,
                 r_o, v_o, kk_o, bonus_o, lw_o, kd_o, kb_o, *, ctx_blocks, n_blocks):
    j = pl.program_id(1)
    tm = zr_ref.shape[0]
    has_prev = jnp.logical_and(j != 0, j != ctx_blocks)
    has_next = jnp.logical_and(j != ctx_blocks - 1, j != n_blocks - 1)
    rows = lax.broadcasted_iota(jnp.int32, zr_ref.shape, 0)

    def conv(x_ref, p_ref, n_ref, w_ref):
        x = x_ref[...]
        w = w_ref[...]
        before = jnp.where(has_prev, p_ref[SUBLANES - 1:SUBLANES, :], 0.0)
        after = jnp.where(has_next, n_ref[0:1, :], 0.0)
        xm = jnp.where(rows == 0, before, pltpu.roll(x, 1, 0))
        xp = jnp.where(rows == tm - 1, after, pltpu.roll(x, tm - 1, 0))
        return xm * w[0:1] + x * w[1:2] + xp * w[2:3]

    r = conv(zr_ref, zrp_ref, zrn_ref, cwr_ref)
    k = conv(zk_ref, zkp_ref, zkn_ref, cwk_ref)
    v = conv(zv_ref, zvp_ref, zvn_ref, cwv_ref)
    ones_bd = _head_ones()

    kx = k * kk_ref[...]
    kkn = kx / jnp.maximum(jnp.sqrt(_head_sum(kx * kx, ones_bd)), 1e-12)
    r_o[...] = r.astype(BF16)
    v_o[...] = v.astype(BF16)
    kk_o[...] = kkn.astype(BF16)
    bonus_o[...] = _head_sum(r * k * rk_ref[...], ones_bd) * v

    zwa = zwa_ref[...]
    lw = _split(jnp.tanh(zwa[:, :LORA_SLOT]), 2)
    la = _split(zwa[:, LORA_SLOT:], 2)
    ka = ka_ref[...]

    def lora(x, w_ref, d):
        out = jnp.dot(x[0], w_ref[d, 0], preferred_element_type=F32)
        out = out + jnp.dot(x[0], w_ref[d, 1], preferred_element_type=F32)
        return out + jnp.dot(x[1], w_ref[d, 0], preferred_element_type=F32)

    for d in range(2):
        xw = w0_ref[d:d + 1, :] + lora(lw, w2_ref, d)
        lw_o[d] = -DECAY_SCALE * jax.nn.sigmoid(xw)
        a = jax.nn.sigmoid(a0_ref[d:d + 1, :] + lora(la, a2_ref, d))
        kd_o[d] = (k * (1.0 + (a - 1.0) * ka)).astype(BF16)
        kb_o[d] = (kkn * a).astype(BF16)


def _rwkv_prep(z, lay, n_ctx, conv_w, w0, w2, a0, a2, k_k, k_a, r_k):
    bsz, t_all, _ = z.shape
    bw = lay["bw"]
    tm = 256
    cw = _pick(bw, (512, 256, 128))
    nblk = t_all // tm
    cb = n_ctx // tm
    sub = tm // SUBLANES
    last8 = t_all // SUBLANES - 1

    def main(off):
        return pl.BlockSpec((None, tm, cw), lambda b, j, c: (b, j, off // cw + c))

    def before(off):
        return pl.BlockSpec((None, SUBLANES, cw), lambda b, j, c: (b, jnp.maximum(j * sub - 1, 0), off // cw + c))

    def after(off):
        return pl.BlockSpec((None, SUBLANES, cw), lambda b, j, c: (b, jnp.minimum((j + 1) * sub, last8), off // cw + c))

    def chan(rows, off=0):
        return pl.BlockSpec((rows, cw), lambda b, j, c: (0, off // cw + c))

    def pad_lora(w):
        full = jnp.zeros((2, LORA_SLOT, bw), F32)
        for d in range(2):
            full = full.at[d, d * LORA_RANK:(d + 1) * LORA_RANK].set(w[d])
        hi = full.astype(BF16)
        lo = (full - hi.astype(F32)).astype(BF16)
        return jnp.stack([hi, lo], axis=1)

    lora_spec = pl.BlockSpec((2, 2, LORA_SLOT, cw), lambda b, j, c: (0, 0, 0, c))
    tok = pl.BlockSpec((None, tm, cw), lambda b, j, c: (b, j, c))
    tok2 = pl.BlockSpec((2, None, tm, cw), lambda b, j, c: (0, b, j, c))
    one = jax.ShapeDtypeStruct((bsz, t_all, bw), BF16)
    one32 = jax.ShapeDtypeStruct((bsz, t_all, bw), F32)
    two = jax.ShapeDtypeStruct((2, bsz, t_all, bw), BF16)
    two32 = jax.ShapeDtypeStruct((2, bsz, t_all, bw), F32)
    return pl.pallas_call(
        functools.partial(_prep_kernel, ctx_blocks=cb, n_blocks=nblk),
        grid=(bsz, nblk, bw // cw),
        in_specs=[
            main(lay["r"]), main(lay["kb"]), main(lay["vb"]),
            before(lay["r"]), before(lay["kb"]), before(lay["vb"]),
            after(lay["r"]), after(lay["kb"]), after(lay["vb"]),
            pl.BlockSpec((None, tm, 2 * LORA_SLOT), lambda b, j, c: (b, j, lay["lora"] // (2 * LORA_SLOT))),
            chan(3, 0), chan(3, bw), chan(3, 2 * bw),
            lora_spec, lora_spec,
            chan(2), chan(2), chan(1), chan(1), chan(1),
        ],
        out_specs=[tok, tok, tok, tok, tok2, tok2, tok2],
        out_shape=[one, one, one, one32, two32, two, two],
        compiler_params=_params("parallel", "parallel", "parallel"),
        name="rwkv_prep",
    )(z, z, z, z, z, z, z, z, z, z, conv_w, conv_w, conv_w, pad_lora(w2), pad_lora(a2),
      w0, a0, k_k.reshape(1, bw), k_a.reshape(1, bw), r_k.reshape(1, bw))


def _stack_heads(x, lane_is_first):
    return jnp.concatenate([jnp.where(lane_is_first, x, 0.0), jnp.where(lane_is_first, 0.0, x)], axis=0)


def _wkv_operands(r, v, kk, lw, kd, kb, reverse):
    ch = SCAN_CHUNK
    row = lax.broadcasted_iota(jnp.int32, (ch, ch), 0)
    col = lax.broadcasted_iota(jnp.int32, (ch, ch), 1)
    incl = (row <= col) if reverse else (row >= col)
    cum = _dotp(incl.astype(F32), lw, ta=1, tb=3)
    last = cum[0:1, :] if reverse else cum[ch - 1:ch, :]
    g_in = jnp.exp(cum)
    g_inv = jnp.exp(-cum)
    g_ex = jnp.exp(cum - lw)
    g_rem = jnp.exp(last - cum)
    g_end = jnp.exp(last)
    a_t = -kk * g_ex
    b_t = kb * g_inv
    k_t = kd * g_inv
    r_t = r * g_in
    k_g = kd * g_rem
    b_g = kb * g_rem
    return dict(a=a_t, r=r_t, b=b_t, k=k_t, v=v, kg=k_g, bg=b_g, g_end=g_end)


def _wkv_step(streams, terms):
    ch = SCAN_CHUNK
    two = 2 * ch
    dot = functools.partial(_dotp, ta=terms, tb=terms)
    i2 = lax.broadcasted_iota(jnp.int32, (two, two), 0) % ch
    j2 = lax.broadcasted_iota(jnp.int32, (two, two), 1) % ch
    first = lax.broadcasted_iota(jnp.int32, (ch, LANES), 1) < B_HEAD_DIM
    st = functools.partial(_stack_heads, lane_is_first=first)

    chains = []
    for ops, s_ref, y_ref, reverse in streams:
        m_incl = (i2 <= j2) if reverse else (i2 >= j2)
        m_strict = (i2 < j2) if reverse else (i2 > j2)
        for p in range(ops["a"].shape[1] // LANES):
            sl = slice(p * LANES, (p + 1) * LANES)
            c = {n: st(ops[n][:, sl]) for n in ("a", "r", "b", "k", "v", "kg", "bg")}
            c.update(sl=sl, p=p, s_ref=s_ref, y_ref=y_ref, m_incl=m_incl, m_strict=m_strict,
                     g_end=ops["g_end"][:, sl])
            chains.append(c)

    four = 2 * two
    for c in chains:
        c["state"] = c["s_ref"][c["p"]]
        lhs = jnp.concatenate([c["a"], c["r"]], axis=0)
        res = dot(lhs, jnp.concatenate([c["b"], c["k"], c["state"]], axis=0), _NT)
        c["dmat"] = res[:, :four]
        c["from_state"] = res[:, four:]
    for c in chains:
        d = c["dmat"]
        c["d_ab"] = jnp.where(c["m_strict"], d[:two, :two], 0.0)
        c["d_rb"] = jnp.where(c["m_incl"], d[two:, :two], 0.0)
        c["d_rk"] = jnp.where(c["m_incl"], d[two:, two:], 0.0)
        c["x"] = c["from_state"][:two] + dot(jnp.where(c["m_strict"], d[:two, two:], 0.0), c["v"])
    for lvl in range(SCAN_LEVELS):
        for c in chains:
            if lvl + 1 < SCAN_LEVELS:
                res = dot(c["d_ab"], jnp.concatenate([c["d_ab"], c["x"]], axis=1))
                c["d_ab"] = res[:, :two]
                c["x"] = c["x"] + res[:, two:]
            else:
                c["x"] = c["x"] + dot(c["d_ab"], c["x"])
    for c in chains:
        c["vu"] = jnp.concatenate([c["v"], c["x"]], axis=0)
        y_s = c["from_state"][two:] + dot(jnp.concatenate([c["d_rk"], c["d_rb"]], axis=1), c["vu"])
        c["y_ref"][:, c["sl"]] = y_s[:ch] + y_s[ch:]
    for c in chains:
        c["s_ref"][c["p"]] = c["state"] * c["g_end"] + dot(
            c["vu"], jnp.concatenate([c["kg"], c["bg"]], axis=0), _TN)


def _wkv_kernel(rf_ref, vf_ref, kkf_ref, lwf_ref, kdf_ref, kbf_ref,
                rb_ref, vb_ref, kkb_ref, lwb_ref, kdb_ref, kbb_ref,
                yf_ref, yb_ref, sf_ref, sb_ref, *, terms):
    @pl.when(pl.program_id(2) == 0)
    def _():
        sf_ref[...] = jnp.zeros_like(sf_ref)
        sb_ref[...] = jnp.zeros_like(sb_ref)

    def f32(ref):
        return ref[...].astype(F32)

    fwd = _wkv_operands(f32(rf_ref), f32(vf_ref), f32(kkf_ref), lwf_ref[...], f32(kdf_ref), f32(kbf_ref), False)
    bwd = _wkv_operands(f32(rb_ref), f32(vb_ref), f32(kkb_ref), lwb_ref[...], f32(kdb_ref), f32(kbb_ref), True)
    _wkv_step([(fwd, sf_ref, yf_ref, False), (bwd, sb_ref, yb_ref, True)], terms)


def _wkv(r, v, kk, lw, kd, kb, n_ctx, terms):
    bsz, t_all, bw = r.shape
    ch = SCAN_CHUNK
    gw = _pick(bw, (SCAN_GROUP_WIDTH, 256, 128))
    nc = t_all // ch
    ncc = n_ctx // ch

    def fwd(c):
        return c

    def bwd(c):
        return jnp.where(c < ncc, ncc - 1 - c, nc + ncc - 1 - c)

    def tok(order):
        return pl.BlockSpec((None, ch, gw), lambda b, g, c: (b, order(c), g))

    def tok2(order, d):
        return pl.BlockSpec((None, None, ch, gw), lambda b, g, c: (d, b, order(c), g))

    out = jax.ShapeDtypeStruct((bsz, t_all, bw), F32)
    state = pltpu.VMEM((gw // LANES, LANES, LANES), F32)
    return pl.pallas_call(
        functools.partial(_wkv_kernel, terms=terms),
        grid=(bsz, bw // gw, nc),
        in_specs=[tok(fwd), tok(fwd), tok(fwd), tok2(fwd, 0), tok2(fwd, 0), tok2(fwd, 0),
                  tok(bwd), tok(bwd), tok(bwd), tok2(bwd, 1), tok2(bwd, 1), tok2(bwd, 1)],
        out_specs=[tok(fwd), tok(bwd)],
        out_shape=[out, out],
        scratch_shapes=[state, state],
        compiler_params=_params("parallel", "parallel", "arbitrary"),
        name="wkv_scan",
    )(r, v, kk, lw, kd, kb, r, v, kk, lw, kd, kb)


def _readout_kernel(yf_ref, yb_ref, bonus_ref, gate_ref, gw_ref, gb_ref, o_ref):
    ones_bd = _head_ones()
    y = yf_ref[...] + yb_ref[...]
    inv = 1.0 / B_HEAD_DIM
    mean = _head_sum(y, ones_bd) * inv
    yc = y - mean
    var = _head_sum(yc * yc, ones_bd) * inv
    out = yc * lax.rsqrt(var + GN_EPS) * gw_ref[...] + gb_ref[...]
    out = out + bonus_ref[...]
    o_ref[...] = (out * _silu(gate_ref[...])).astype(BF16)


def _readout(yf, yb, bonus, z, lay, n_ctx, gn_w, gn_b):
    bsz, t_all, bw = yf.shape
    n_lat = t_all - n_ctx
    tm = 256
    cw = _pick(bw, (512, 256, 128))
    off = n_ctx // tm
    tok = pl.BlockSpec((None, tm, cw), lambda b, j, c: (b, j + off, c))
    chan = pl.BlockSpec((1, cw), lambda b, j, c: (0, c))
    return pl.pallas_call(
        _readout_kernel,
        grid=(bsz, n_lat // tm, bw // cw),
        in_specs=[tok, tok, tok,
                  pl.BlockSpec((None, tm, cw), lambda b, j, c: (b, j + off, lay["gb"] // cw + c)),
                  chan, chan],
        out_specs=pl.BlockSpec((None, tm, cw), lambda b, j, c: (b, j, c)),
        out_shape=jax.ShapeDtypeStruct((bsz, n_lat, bw), BF16),
        compiler_params=_params("parallel", "parallel", "parallel"),
        name="rwkv_readout",
    )(yf, yb, bonus, z, gn_w.reshape(1, bw), gn_b.reshape(1, bw))


def _out_proj_kernel(a_ref, b_ref, wa_ref, wb_ref, x_ref, gate_ref, o_ref):
    y = jnp.dot(a_ref[...], wa_ref[...], preferred_element_type=F32)
    y = y + jnp.dot(b_ref[...], wb_ref[...], preferred_element_type=F32)
    o_ref[...] = x_ref[...] + gate_ref[...] * y


def _out_proj(mix_a, mix_b, w_a, w_b, x, mod):
    bsz, t, d = x.shape
    ka, kb = mix_a.shape[2], mix_b.shape[2]
    tm = _pick(t, (1024, 512, 256))
    tn = _pick(d, (512, 256, 128))
    mod3 = mod.reshape(mod.shape[0], 1, 3 * d)
    return pl.pallas_call(
        _out_proj_kernel,
        grid=(bsz, t // tm, d // tn),
        in_specs=[
            pl.BlockSpec((None, tm, ka), lambda b, i, j: (b, i, 0)),
            pl.BlockSpec((None, tm, kb), lambda b, i, j: (b, i, 0)),
            pl.BlockSpec((ka, tn), lambda b, i, j: (0, j)),
            pl.BlockSpec((kb, tn), lambda b, i, j: (0, j)),
            pl.BlockSpec((None, tm, tn), lambda b, i, j: (b, i, j)),
            pl.BlockSpec((None, 1, tn), lambda b, i, j: (b, 0, 2 * (d // tn) + j)),
        ],
        out_specs=pl.BlockSpec((None, tm, tn), lambda b, i, j: (b, i, j)),
        out_shape=jax.ShapeDtypeStruct((bsz, t, d), F32),
        compiler_params=_params("parallel", "parallel", "parallel"),
        name="out_proj",
    )(mix_a, mix_b, w_a, w_b, x, mod3)


def _fold_lanes(x):
    out = x[:, :LANES]
    for i in range(1, x.shape[1] // LANES):
        out = out + x[:, i * LANES:(i + 1) * LANES]
    return out


def _gm_in_kernel(h_ref, wu_ref, wv_ref, wg_ref, p_ref, gv_ref, s1_ref, s2_ref):
    j = pl.program_id(1)
    h = h_ref[...]
    u = jnp.dot(h, wu_ref[...], preferred_element_type=F32)
    g = jnp.dot(h, wg_ref[...], preferred_element_type=F32)
    p_ref[...] = (jax.nn.gelu(u) * _silu(g)).astype(BF16)
    gv = jax.nn.gelu(jnp.dot(h, wv_ref[...], preferred_element_type=F32))
    gv_ref[...] = gv

    @pl.when(j == 0)
    def _():
        s1_ref[...] = jnp.zeros_like(s1_ref)
        s2_ref[...] = jnp.zeros_like(s2_ref)

    s1_ref[...] += _fold_lanes(gv)
    s2_ref[...] += _fold_lanes(gv * gv)


def _gm_in(h, w_in, cwid):
    m, d = h.shape
    tm = _pick(m, (1024, 512, 256))
    tn = _pick(cwid, (256, 128))
    nt = cwid // tn
    stat = jax.ShapeDtypeStruct((m, LANES), F32)
    return pl.pallas_call(
        _gm_in_kernel,
        grid=(m // tm, nt),
        in_specs=[
            pl.BlockSpec((tm, d), lambda i, j: (i, 0)),
            pl.BlockSpec((d, tn), lambda i, j: (0, j)),
            pl.BlockSpec((d, tn), lambda i, j: (0, nt + j)),
            pl.BlockSpec((d, tn), lambda i, j: (0, 2 * nt + j)),
        ],
        out_specs=[
            pl.BlockSpec((tm, tn), lambda i, j: (i, j)),
            pl.BlockSpec((tm, tn), lambda i, j: (i, j)),
            pl.BlockSpec((tm, LANES), lambda i, j: (i, 0)),
            pl.BlockSpec((tm, LANES), lambda i, j: (i, 0)),
        ],
        out_shape=[jax.ShapeDtypeStruct((m, cwid), BF16), jax.ShapeDtypeStruct((m, cwid), F32), stat, stat],
        compiler_params=_params("parallel", "arbitrary"),
        name="gmlp_in",
    )(h, w_in, w_in, w_in)


def _gm_out_kernel(gv_ref, p_ref, s1_ref, s2_ref, lng_ref, lnb_ref, ws_ref, bs_ref, wo_ref, o_ref, y_ref,
                   *, cwid, groups):
    s = pl.program_id(0)
    j = pl.program_id(1)
    tm = gv_ref.shape[0]

    @pl.when(jnp.logical_and(s == 0, j == 0))
    def _():
        y_ref[...] = jnp.zeros_like(y_ref)

    inv = 1.0 / cwid
    mean = jnp.sum(s1_ref[...], axis=-1, keepdims=True) * inv
    var = jnp.sum(s2_ref[...], axis=-1, keepdims=True) * inv - mean * mean
    rstd = lax.rsqrt(var + LN_EPS)
    vln = ((gv_ref[...] - mean) * rstd * lng_ref[...] + lnb_ref[...]).astype(BF16)
    ws = ws_ref[...].astype(BF16)
    bias = bs_ref[...]
    p = p_ref[...]
    ys = []
    for c in range(tm // C_CHUNK):
        rows = slice(c * C_CHUNK, (c + 1) * C_CHUNK)
        vm = jnp.dot(ws, vln[rows], preferred_element_type=F32) + bias
        ys.append((p[rows].astype(F32) * vm).astype(BF16))
    y_ref[s % 2, j] = jnp.concatenate(ys, axis=0)

    prev = (s + 1) % 2
    acc = jnp.dot(y_ref[prev, 0], wo_ref[0], preferred_element_type=F32)
    for g in range(1, groups):
        acc = acc + jnp.dot(y_ref[prev, g], wo_ref[g], preferred_element_type=F32)
    o_ref[...] = acc


def _gm_out(gv, p, s1, s2, ln_g, ln_b, w_s, b_s, w_out):
    m, cwid = gv.shape
    groups = w_s.shape[0]
    gwid = cwid // groups
    d = w_out.shape[1]
    tm = _pick(m, (512, 256))
    nrb = m // tm
    assert d % groups == 0 and (d // groups) % LANES == 0
    tn = d // groups

    def row(s):
        return jnp.minimum(s, nrb - 1)

    def out_idx(s, j):
        return (jnp.maximum(s - 1, 0), jnp.where(s == 0, 0, j))

    return pl.pallas_call(
        functools.partial(_gm_out_kernel, cwid=cwid, groups=groups),
        grid=(nrb + 1, groups),
        in_specs=[
            pl.BlockSpec((tm, gwid), lambda s, j: (row(s), j)),
            pl.BlockSpec((tm, gwid), lambda s, j: (row(s), j)),
            pl.BlockSpec((tm, LANES), lambda s, j: (row(s), 0)),
            pl.BlockSpec((tm, LANES), lambda s, j: (row(s), 0)),
            pl.BlockSpec((1, gwid), lambda s, j: (0, j)),
            pl.BlockSpec((1, gwid), lambda s, j: (0, j)),
            pl.BlockSpec((None, C_CHUNK, C_CHUNK), lambda s, j: (j, 0, 0)),
            pl.BlockSpec((None, C_CHUNK, 1), lambda s, j: (j, 0, 0)),
            pl.BlockSpec((groups, gwid, tn), lambda s, j: (0, 0, j)),
        ],
        out_specs=pl.BlockSpec((tm, tn), out_idx),
        out_shape=jax.ShapeDtypeStruct((m, d), F32),
        scratch_shapes=[pltpu.VMEM((2, groups, tm, gwid), BF16)],
        compiler_params=_params("arbitrary", "arbitrary"),
        name="gmlp_out",
    )(gv, p, s1, s2, ln_g.reshape(1, cwid), ln_b.reshape(1, cwid), w_s,
      b_s.reshape(groups, C_CHUNK, 1), w_out.reshape(groups, gwid, d))


def _final_kernel(x_ref, y_ref, gate_ref, g_ref, o_ref):
    x = x_ref[...] + gate_ref[...] * y_ref[...]
    ms = jnp.mean(x * x, axis=-1, keepdims=True)
    o_ref[...] = (x * lax.rsqrt(ms + NORM_EPS)) * g_ref[...]


def _final(x, y, mod, final_g):
    bsz, t, d = x.shape
    tm = 256
    mod3 = mod.reshape(mod.shape[0], 1, 3 * d)
    tok = pl.BlockSpec((None, tm, d), lambda b, j: (b, j, 0))
    return pl.pallas_call(
        _final_kernel,
        grid=(bsz, t // tm),
        in_specs=[tok, tok,
                  pl.BlockSpec((None, 1, d), lambda b, j: (b, 0, 2)),
                  pl.BlockSpec((1, d), lambda b, j: (0, 0))],
        out_specs=tok,
        out_shape=jax.ShapeDtypeStruct((bsz, t, d), F32),
        compiler_params=_params("parallel", "parallel"),
        name="final_norm",
    )(x, y.reshape(bsz, t, d), mod3, final_g.reshape(1, d))


def _ab_layout(d, heads, ab_in):
    aq = heads * A_HEAD_DIM
    bw = d // 2
    akv = (ab_in - 2 * aq - 4 * bw - 4 * LORA_RANK) // 2
    lay = {"aq": aq, "akv": akv, "bw": bw}
    off = 0
    for name, width in (("q", aq), ("ga", aq), ("r", bw), ("kb", bw), ("vb", bw), ("gb", bw),
                        ("lora", 2 * LORA_SLOT), ("k", akv), ("v", akv)):
        lay[name] = off
        off += width
    lay["width"] = off
    return lay


def _ab_weight(w_in, lay):
    aq, akv, bw = lay["aq"], lay["akv"], lay["bw"]
    widths = (aq, akv, akv, aq, bw, bw, bw, bw, 2 * LORA_RANK, 2 * LORA_RANK)
    parts = []
    off = 0
    for w in widths:
        parts.append(w_in[:, off:off + w])
        off += w
    q, k, v, ga, r, kb, vb, gb, lw, la = parts
    q = _rope_head_order(q, aq // A_HEAD_DIM)
    k = _rope_head_order(k, akv // A_HEAD_DIM)
    pad =jnp.zeros((w_in.shape[0], LORA_SLOT - 2 * LORA_RANK), w_in.dtype)
    return jnp.concatenate([q, ga, r, kb, vb, gb, lw, pad, la, pad, k, v], axis=1).astype(BF16)


def kernel(x, c, ctx, c_ctx, mod_w, mod_b, norm_g, ab_w_in, ab_w_out, attn_sink, rwkv_conv, rwkv_w0, rwkv_w2,
           rwkv_a0, rwkv_a2, rwkv_k_k, rwkv_k_a, rwkv_r_k, rwkv_gn_w, rwkv_gn_b, gm_w_in, gm_ln_g, gm_ln_b,
           gm_w_s, gm_b_s, gm_w_out, final_g):
    bsz, n_lat, d = x.shape
    n_ctx = ctx.shape[1]
    heads = attn_sink.shape[1]

    cvec = jnp.concatenate([c, c_ctx[None, :], jnp.zeros((SUBLANES - bsz - 1, d), F32)], axis=0)
    mod = _modulation(cvec, mod_w, mod_b)

    lay = _ab_layout(d, heads, ab_w_in.shape[2])
    w_in0 = _ab_weight(ab_w_in[0], lay)
    h0 = _norm_mod(x, ctx, norm_g[0], mod[0], bsz)
    t_all = n_ctx + n_lat
    z = _matmul(h0.reshape(bsz * t_all, d), w_in0).reshape(bsz, t_all, lay["width"])
    mix_a = _attention(z, attn_sink[0], _rope_tables(n_lat), lay, n_ctx, n_lat)
    r, v, kk, bonus, lw, kd, kb = _rwkv_prep(z, lay, n_ctx, rwkv_conv[0], rwkv_w0[0], rwkv_w2[0], rwkv_a0[0],
                                             rwkv_a2[0], rwkv_k_k[0], rwkv_k_a[0], rwkv_r_k[0])
    yf, yb = _wkv(r, v, kk, lw, kd, kb, n_ctx, terms=1)
    mix_b = _readout(yf, yb, bonus, z, lay, n_ctx, rwkv_gn_w[0], rwkv_gn_b[0])
    w_out0 = ab_w_out[0].astype(BF16)
    x1 = _out_proj(mix_a, mix_b, w_out0[:lay["aq"]], w_out0[lay["aq"]:], x, mod[0])

    cwid = gm_ln_g.shape[1]
    h1 = _norm_mod(x1, None, norm_g[1], mod[1], 0)
    p, gv, s1, s2 = _gm_in(h1.reshape(bsz * n_lat, d), gm_w_in[0].astype(BF16), cwid)
    y1 = _gm_out(gv, p, s1, s2, gm_ln_g[0], gm_ln_b[0], gm_w_s[0], gm_b_s[0], gm_w_out[0].astype(BF16))
    return _final(x1, y1, mod[1], final_g)
```

```python
import functools
import math

import jax
import jax.numpy as jnp
from jax import lax
from jax.experimental import pallas as pl
from jax.experimental.pallas import tpu as pltpu

F32 = jnp.float32
BF16 = jnp.bfloat16

A_HEAD_DIM = 128
A_WINDOW = 128
A_BLOCK = 128
GRID_W = 64
ROPE_THETA = 10000.0
B_HEAD_DIM = 64
LORA_RANK = 96
GN_EPS = 64e-5
C_CHUNK = 128
LN_EPS = 1e-5
NORM_EPS = 1e-6

LANES = 128
SUBLANES = 8
VMEM_LIMIT_BYTES = 56 * 1024 * 1024

SCAN_CHUNK = 64
SCAN_LEVELS = 6
SCAN_GROUP_WIDTH = 2048
LORA_SLOT = 256
MASK_VALUE = -1e30
DECAY_SCALE = math.exp(-0.5)


def _params(*semantics):
    return pltpu.CompilerParams(dimension_semantics=semantics, vmem_limit_bytes=VMEM_LIMIT_BYTES)


def _pick(n, candidates):
    for c in candidates:
        if n % c == 0:
            return c
    raise ValueError(f"no tile in {candidates} divides {n}")


def _split(x, terms):
    parts = []
    rest = x
    for i in range(terms):
        p = rest.astype(BF16)
        parts.append(p)
        if i + 1 < terms:
            rest = rest - p.astype(F32)
    return parts


_NN = (((1,), (0,)), ((), ()))
_NT = (((1,), (1,)), ((), ()))
_TN = (((0,), (0,)), ((), ()))


def _dotp(a, b, dims=_NN, ta=1, tb=1):
    pa = _split(a, ta)
    pb = _split(b, tb)
    out = None
    for i in range(ta):
        for j in range(tb):
            if i + j >= max(ta, tb):
                continue
            t = lax.dot_general(pa[i], pb[j], dims, preferred_element_type=F32)
            out = t if out is None else out + t
    return out


def _silu(x):
    return x * jax.nn.sigmoid(x)


def _mod_kernel(c_ref, w_ref, b_ref, o_ref):
    s = _silu(c_ref[...])
    o_ref[...] = _dotp(s, w_ref[...], ta=2, tb=2) + b_ref[...]


def _modulation(cvec, mod_w, mod_b):
    depth, d, n = mod_w.shape
    rows = cvec.shape[0]
    tn = _pick(n, (512, 256, 128))
    return pl.pallas_call(
        _mod_kernel,
        grid=(depth, n // tn),
        in_specs=[
            pl.BlockSpec((rows, d), lambda l, j: (0, 0)),
            pl.BlockSpec((None, d, tn), lambda l, j: (l, 0, j)),
            pl.BlockSpec((None, 1, tn), lambda l, j: (l, 0, j)),
        ],
        out_specs=pl.BlockSpec((None, rows, tn), lambda l, j: (l, 0, j)),
        out_shape=jax.ShapeDtypeStruct((depth, rows, n), F32),
        compiler_params=_params("parallel", "parallel"),
        name="modulation",
    )(cvec, mod_w, mod_b.reshape(depth, 1, n))


def _norm_rows(x, g, shift, scale):
    ms = jnp.mean(x * x, axis=-1, keepdims=True)
    y = x * lax.rsqrt(ms + NORM_EPS)
    return (y * g) * (1.0 + scale) + shift


def _norm_mod_kernel(x_ref, c_ref, g_ref, shift_ref, scale_ref, o_ref, *, ctx_blocks):
    j = pl.program_id(1)

    @pl.when(j < ctx_blocks)
    def _():
        o_ref[...] = _norm_rows(c_ref[...], g_ref[...], shift_ref[...], scale_ref[...]).astype(BF16)

    @pl.when(j >= ctx_blocks)
    def _():
        o_ref[...] = _norm_rows(x_ref[...], g_ref[...], shift_ref[...], scale_ref[...]).astype(BF16)


def _norm_mod(x, ctx, g, mod, ctx_row):
    bsz, t, d = x.shape
    tm = 256
    if ctx is None:
        ctx = x
        cb = 0
    else:
        cb = ctx.shape[1] // tm
    nblk = cb + t // tm
    rows = mod.shape[0]
    mod3 = mod.reshape(rows, 1, 3 * d)

    def mrow(b, j):
        return jnp.where(j < cb, ctx_row, b)

    return pl.pallas_call(
        functools.partial(_norm_mod_kernel, ctx_blocks=cb),
        grid=(bsz, nblk),
        in_specs=[
            pl.BlockSpec((None, tm, d), lambda b, j: (b, jnp.maximum(j - cb, 0), 0)),
            pl.BlockSpec((None, tm, d), lambda b, j: (b, jnp.minimum(j, max(cb - 1, 0)), 0)),
            pl.BlockSpec((1, d), lambda b, j: (0, 0)),
            pl.BlockSpec((None, 1, d), lambda b, j: (mrow(b, j), 0, 0)),
            pl.BlockSpec((None, 1, d), lambda b, j: (mrow(b, j), 0, 1)),
        ],
        out_specs=pl.BlockSpec((None, tm, d), lambda b, j: (b, j, 0)),
        out_shape=jax.ShapeDtypeStruct((bsz, nblk * tm, d), BF16),
        compiler_params=_params("parallel", "parallel"),
        name="norm_mod",
    )(x, ctx, g.reshape(1, d), mod3, mod3)


def _mm_kernel(a_ref, b_ref, o_ref):
    o_ref[...] = jnp.dot(a_ref[...], b_ref[...], preferred_element_type=F32)


def _matmul(a, b):
    m, k = a.shape
    n = b.shape[1]
    tm = _pick(m, (1536, 1024, 768, 512, 256))
    tn = _pick(n, (512, 256, 128))
    return pl.pallas_call(
        _mm_kernel,
        grid=(m // tm, n // tn),
        in_specs=[
            pl.BlockSpec((tm, k), lambda i, j: (i, 0)),
            pl.BlockSpec((k, tn), lambda i, j: (0, j)),
        ],
        out_specs=pl.BlockSpec((tm, tn), lambda i, j: (i, j)),
        out_shape=jax.ShapeDtypeStruct((m, n), F32),
        compiler_params=_params("parallel", "parallel"),
        name="in_proj",
    )(a, b)


def _rope(x, cos, sin_signed):
    out = []
    for h in range(x.shape[1] // A_HEAD_DIM):
        xh = x[:, h * A_HEAD_DIM:(h + 1) * A_HEAD_DIM]
        out.append(xh * cos + pltpu.roll(xh, A_HEAD_DIM // 2, 1) * sin_signed)
    return out


def _rope_head_order(w, n_heads):
    w5 = w.reshape(w.shape[0], n_heads, 2, 2, A_HEAD_DIM // 4)
    return jnp.swapaxes(w5, 2, 3).reshape(w.shape)


def _attn_kernel(sink_ref, q_ref, g_ref, kp_ref, kc_ref, kn_ref, vp_ref, vc_ref, vn_ref,
                 kx_ref, vx_ref, cq_ref, sq_ref, cp_ref, sp_ref, cn_ref, sn_ref, o_ref,
                 *, n_blocks, kv_heads, group):
    n = pl.program_id(1)
    blk = A_BLOCK
    lctx = kx_ref.shape[0]
    scale = A_HEAD_DIM ** -0.5

    q_heads = _rope(q_ref[...], cq_ref[...], sq_ref[...])
    kp = _rope(kp_ref[...], cp_ref[...], sp_ref[...])
    kc = _rope(kc_ref[...], cq_ref[...], sq_ref[...])
    kn = _rope(kn_ref[...], cn_ref[...], sn_ref[...])

    qi = lax.broadcasted_iota(jnp.int32, (blk, lctx + 3 * blk), 0)
    kj = lax.broadcasted_iota(jnp.int32, (blk, lctx + 3 * blk), 1) - lctx
    no_prev = jnp.where(n > 0, 0, blk)
    no_next = jnp.where(n < n_blocks - 1, 0, blk)
    prev_ok = (kj >= qi + no_prev) & (kj < blk)
    next_ok = (kj - 2 * blk <= qi - no_next) & (kj >= 2 * blk)
    valid = (kj < 0) | prev_ok | ((kj >= blk) & (kj < 2 * blk)) | next_ok
    mask = jnp.where(valid, 0.0, MASK_VALUE).astype(F32)
    mask = jnp.concatenate([mask] * group, axis=0)

    gate = g_ref[...]
    for hk in range(kv_heads):
        sl = slice(hk * A_HEAD_DIM, (hk + 1) * A_HEAD_DIM)
        kcat = jnp.concatenate([kx_ref[:, sl], kp[hk], kc[hk], kn[hk]], axis=0).astype(BF16)
        vcat = jnp.concatenate([vx_ref[:, sl], vp_ref[:, sl], vc_ref[:, sl], vn_ref[:, sl]], axis=0).astype(BF16)
        qs = jnp.concatenate([q_heads[hk * group + g] * scale for g in range(group)], axis=0).astype(BF16)
        s = lax.dot_general(qs, kcat, _NT, preferred_element_type=F32) + mask
        sink = jnp.concatenate(
            [jnp.full((blk, 1), sink_ref[hk * group + g], F32) for g in range(group)], axis=0)
        m = jnp.maximum(jnp.max(s, axis=-1, keepdims=True), sink)
        p = jnp.exp(s - m)
        den = jnp.exp(sink - m) + jnp.sum(p, axis=-1, keepdims=True)
        o = jnp.dot(p.astype(BF16), vcat, preferred_element_type=F32) / den
        for g in range(group):
            h = hk * group + g
            hs = slice(h * A_HEAD_DIM, (h + 1) * A_HEAD_DIM)
            o_ref[:, hs] = (o[g * blk:(g + 1) * blk] * _silu(gate[:, hs])).astype(BF16)


def _attention(z, sink, tables, lay, n_ctx, n_lat):
    bsz = z.shape[0]
    aq, akv = lay["aq"], lay["akv"]
    heads = aq // A_HEAD_DIM
    kv_heads = akv // A_HEAD_DIM
    blk = A_BLOCK
    nb = n_lat // blk
    off = n_ctx // blk
    kcol = lay["k"] // akv
    vcol = lay["v"] // akv
    cos, sin = tables

    def prev(n):
        return jnp.maximum(n - 1, 0)

    def nxt(n):
        return jnp.minimum(n + 1, nb - 1)

    def zspec(width, rowf, col):
        return pl.BlockSpec((None, blk, width), lambda b, n: (b, rowf(n) + off, col))

    def tspec(rowf):
        return pl.BlockSpec((blk, A_HEAD_DIM), lambda b, n: (rowf(n), 0))

    ident = lambda n: n
    return pl.pallas_call(
        functools.partial(_attn_kernel, n_blocks=nb, kv_heads=kv_heads, group=heads // kv_heads),
        grid=(bsz, nb),
        in_specs=[
            pl.BlockSpec(memory_space=pltpu.SMEM),
            zspec(aq, ident, lay["q"] // aq),
            zspec(aq, ident, lay["ga"] // aq),
            zspec(akv, prev, kcol), zspec(akv, ident, kcol), zspec(akv, nxt, kcol),
            zspec(akv, prev, vcol), zspec(akv, ident, vcol), zspec(akv, nxt, vcol),
            pl.BlockSpec((None, n_ctx, akv), lambda b, n: (b, 0, kcol)),
            pl.BlockSpec((None, n_ctx, akv), lambda b, n: (b, 0, vcol)),
            tspec(ident), tspec(ident), tspec(prev), tspec(prev), tspec(nxt), tspec(nxt),
        ],
        out_specs=pl.BlockSpec((None, blk, aq), lambda b, n: (b, n, 0)),
        out_shape=jax.ShapeDtypeStruct((bsz, n_lat, aq), BF16),
        compiler_params=_params("parallel", "parallel"),
        name="band_attention",
    )(sink, z, z, z, z, z, z, z, z, z, z, cos, sin, cos, sin, cos, sin)


def _rope_tables(n_tok):
    rows = n_tok // GRID_W
    row = jnp.repeat(jnp.arange(rows, dtype=F32), GRID_W)
    col = jnp.tile(jnp.arange(GRID_W, dtype=F32), rows)
    n_freq = A_HEAD_DIM // 4
    inv_freq = ROPE_THETA ** (-jnp.arange(n_freq, dtype=F32) / n_freq)
    ar = row[:, None] * inv_freq
    ac = col[:, None] * inv_freq
    cos = jnp.concatenate([jnp.cos(ar), jnp.cos(ac), jnp.cos(ar), jnp.cos(ac)], axis=-1)
    sin = jnp.concatenate([-jnp.sin(ar), -jnp.sin(ac), jnp.sin(ar), jnp.sin(ac)], axis=-1)
    return cos, sin


def _head_sum(x, ones_bd):
    out = []
    for i in range(x.shape[1] // LANES):
        out.append(_dotp(x[:, i * LANES:(i + 1) * LANES], ones_bd, ta=2, tb=1))
    return jnp.concatenate(out, axis=1) if len(out) > 1 else out[0]


def _head_ones():
    r = lax.broadcasted_iota(jnp.int32, (LANES, LANES), 0) // B_HEAD_DIM
    c = lax.broadcasted_iota(jnp.int32, (LANES, LANES), 1) // B_HEAD_DIM
    return (r == c).astype(F32)


def _prep_kernel(zr_ref, zk_ref, zv_ref, zrp_ref, zkp_ref, zvp_ref, zrn_ref, zkn_ref, zvn_ref,
                 zwa_ref, cwr_ref, cwk_ref, cwv_ref, w2_ref, a2_ref,
                 w0_ref, a0_ref, kk_ref, ka_ref, rk_ref,
                 r_o, v_o, kk_o, bonus_o, lw_o, kd_o, kb_o, *, ctx_blocks, n_blocks):
    j = pl.program_id(1)
    tm = zr_ref.shape[0]
    has_prev = jnp.logical_and(j != 0, j != ctx_blocks)
    has_next = jnp.logical_and(j != ctx_blocks - 1, j != n_blocks - 1)
    rows = lax.broadcasted_iota(jnp.int32, zr_ref.shape, 0)

    def conv(x_ref, p_ref, n_ref, w_ref):
        x = x_ref[...]
        w = w_ref[...]
        before = jnp.where(has_prev, p_ref[SUBLANES - 1:SUBLANES, :], 0.0)
        after = jnp.where(has_next, n_ref[0:1, :], 0.0)
        xm = jnp.where(rows == 0, before, pltpu.roll(x, 1, 0))
        xp = jnp.where(rows == tm - 1, after, pltpu.roll(x, tm - 1, 0))
        return xm * w[0:1] + x * w[1:2] + xp * w[2:3]

    r = conv(zr_ref, zrp_ref, zrn_ref, cwr_ref)
    k = conv(zk_ref, zkp_ref, zkn_ref, cwk_ref)
    v = conv(zv_ref, zvp_ref, zvn_ref, cwv_ref)
    ones_bd = _head_ones()

    kx = k * kk_ref[...]
    kkn = kx / jnp.maximum(jnp.sqrt(_head_sum(kx * kx, ones_bd)), 1e-12)
    r_o[...] = r.astype(BF16)
    v_o[...] = v.astype(BF16)
    kk_o[...] = kkn.astype(BF16)
    bonus_o[...] = _head_sum(r * k * rk_ref[...], ones_bd) * v

    zwa = zwa_ref[...]
    lw = _split(jnp.tanh(zwa[:, :LORA_SLOT]), 2)
    la = _split(zwa[:, LORA_SLOT:], 2)
    ka = ka_ref[...]

    def lora(x, w_ref, d):
        out = jnp.dot(x[0], w_ref[d, 0], preferred_element_type=F32)
        out = out + jnp.dot(x[0], w_ref[d, 1], preferred_element_type=F32)
        return out + jnp.dot(x[1], w_ref[d, 0], preferred_element_type=F32)

    for d in range(2):
        xw = w0_ref[d:d + 1, :] + lora(lw, w2_ref, d)
        lw_o[d] = -DECAY_SCALE * jax.nn.sigmoid(xw)
        a = jax.nn.sigmoid(a0_ref[d:d + 1, :] + lora(la, a2_ref, d))
        kd_o[d] = (k * (1.0 + (a - 1.0) * ka)).astype(BF16)
        kb_o[d] = (kkn * a).astype(BF16)


def _rwkv_prep(z, lay, n_ctx, conv_w, w0, w2, a0, a2, k_k, k_a, r_k):
    bsz, t_all, _ = z.shape
    bw = lay["bw"]
    tm = 256
    cw = _pick(bw, (512, 256, 128))
    nblk = t_all // tm
    cb = n_ctx // tm
    sub = tm // SUBLANES
    last8 = t_all // SUBLANES - 1

    def main(off):
        return pl.BlockSpec((None, tm, cw), lambda b, j, c: (b, j, off // cw + c))

    def before(off):
        return pl.BlockSpec((None, SUBLANES, cw), lambda b, j, c: (b, jnp.maximum(j * sub - 1, 0), off // cw + c))

    def after(off):
        return pl.BlockSpec((None, SUBLANES, cw), lambda b, j, c: (b, jnp.minimum((j + 1) * sub, last8), off // cw + c))

    def chan(rows, off=0):
        return pl.BlockSpec((rows, cw), lambda b, j, c: (0, off // cw + c))

    def pad_lora(w):
        full = jnp.zeros((2, LORA_SLOT, bw), F32)
        for d in range(2):
            full = full.at[d, d * LORA_RANK:(d + 1) * LORA_RANK].set(w[d])
        hi = full.astype(BF16)
        lo = (full - hi.astype(F32)).astype(BF16)
        return jnp.stack([hi, lo], axis=1)

    lora_spec = pl.BlockSpec((2, 2, LORA_SLOT, cw), lambda b, j, c: (0, 0, 0, c))
    tok = pl.BlockSpec((None, tm, cw), lambda b, j, c: (b, j, c))
    tok2 = pl.BlockSpec((2, None, tm, cw), lambda b, j, c: (0, b, j, c))
    one = jax.ShapeDtypeStruct((bsz, t_all, bw), BF16)
    one32 = jax.ShapeDtypeStruct((bsz, t_all, bw), F32)
    two = jax.ShapeDtypeStruct((2, bsz, t_all, bw), BF16)
    two32 = jax.ShapeDtypeStruct((2, bsz, t_all, bw), F32)
    return pl.pallas_call(
        functools.partial(_prep_kernel, ctx_blocks=cb, n_blocks=nblk),
        grid=(bsz, nblk, bw // cw),
        in_specs=[
            main(lay["r"]), main(lay["kb"]), main(lay["vb"]),
            before(lay["r"]), before(lay["kb"]), before(lay["vb"]),
            after(lay["r"]), after(lay["kb"]), after(lay["vb"]),
            pl.BlockSpec((None, tm, 2 * LORA_SLOT), lambda b, j, c: (b, j, lay["lora"] // (2 * LORA_SLOT))),
            chan(3, 0), chan(3, bw), chan(3, 2 * bw),
            lora_spec, lora_spec,
            chan(2), chan(2), chan(1), chan(1), chan(1),
        ],
        out_specs=[tok, tok, tok, tok, tok2, tok2, tok2],
        out_shape=[one, one, one, one32, two32, two, two],
        compiler_params=_params("parallel", "parallel", "parallel"),
        name="rwkv_prep",
    )(z, z, z, z, z, z, z, z, z, z, conv_w, conv_w, conv_w, pad_lora(w2), pad_lora(a2),
      w0, a0, k_k.reshape(1, bw), k_a.reshape(1, bw), r_k.reshape(1, bw))


def _stack_heads(x, lane_is_first):
    return jnp.concatenate([jnp.where(lane_is_first, x, 0.0), jnp.where(lane_is_first, 0.0, x)], axis=0)


def _wkv_operands(r, v, kk, lw, kd, kb, reverse):
    ch = SCAN_CHUNK
    row = lax.broadcasted_iota(jnp.int32, (ch, ch), 0)
    col = lax.broadcasted_iota(jnp.int32, (ch, ch), 1)
    incl = (row <= col) if reverse else (row >= col)
    cum = _dotp(incl.astype(F32), lw, ta=1, tb=3)
    last = cum[0:1, :] if reverse else cum[ch - 1:ch, :]
    g_in = jnp.exp(cum)
    g_inv = jnp.exp(-cum)
    g_ex = jnp.exp(cum - lw)
    g_rem = jnp.exp(last - cum)
    g_end = jnp.exp(last)
    a_t = -kk * g_ex
    b_t = kb * g_inv
    k_t = kd * g_inv
    r_t = r * g_in
    k_g = kd * g_rem
    b_g = kb * g_rem
    return dict(a=a_t, r=r_t, b=b_t, k=k_t, v=v, kg=k_g, bg=b_g, g_end=g_end)


def _wkv_step(streams, terms):
    ch = SCAN_CHUNK
    two = 2 * ch
    dot = functools.partial(_dotp, ta=terms, tb=terms)
    i2 = lax.broadcasted_iota(jnp.int32, (two, two), 0) % ch
    j2 = lax.broadcasted_iota(jnp.int32, (two, two), 1) % ch
    first = lax.broadcasted_iota(jnp.int32, (ch, LANES), 1) < B_HEAD_DIM
    st = functools.partial(_stack_heads, lane_is_first=first)

    chains = []
    for ops, s_ref, y_ref, reverse in streams:
        m_incl = (i2 <= j2) if reverse else (i2 >= j2)
        m_strict = (i2 < j2) if reverse else (i2 > j2)
        for p in range(ops["a"].shape[1] // LANES):
            sl = slice(p * LANES, (p + 1) * LANES)
            c = {n: st(ops[n][:, sl]) for n in ("a", "r", "b", "k", "v", "kg", "bg")}
            c.update(sl=sl, p=p, s_ref=s_ref, y_ref=y_ref, m_incl=m_incl, m_strict=m_strict,
                     g_end=ops["g_end"][:, sl])
            chains.append(c)

    four = 2 * two
    for c in chains:
        c["state"] = c["s_ref"][c["p"]]
        lhs = jnp.concatenate([c["a"], c["r"]], axis=0)
        res = dot(lhs, jnp.concatenate([c["b"], c["k"], c["state"]], axis=0), _NT)
        c["dmat"] = res[:, :four]
        c["from_state"] = res[:, four:]
    for c in chains:
        d = c["dmat"]
        c["d_ab"] = jnp.where(c["m_strict"], d[:two, :two], 0.0)
        c["d_rb"] = jnp.where(c["m_incl"], d[two:, :two], 0.0)
        c["d_rk"] = jnp.where(c["m_incl"], d[two:, two:], 0.0)
        c["x"] = c["from_state"][:two] + dot(jnp.where(c["m_strict"], d[:two, two:], 0.0), c["v"])
    for lvl in range(SCAN_LEVELS):
        for c in chains:
            if lvl + 1 < SCAN_LEVELS:
                res = dot(c["d_ab"], jnp.concatenate([c["d_ab"], c["x"]], axis=1))
                c["d_ab"] = res[:, :two]
                c["x"] = c["x"] + res[:, two:]
            else:
                c["x"] = c["x"] + dot(c["d_ab"], c["x"])
    for c in chains:
        c["vu"] = jnp.concatenate([c["v"], c["x"]], axis=0)
        y_s = c["from_state"][two:] + dot(jnp.concatenate([c["d_rk"], c["d_rb"]], axis=1), c["vu"])
        c["y_ref"][:, c["sl"]] = y_s[:ch] + y_s[ch:]
    for c in chains:
        c["s_ref"][c["p"]] = c["state"] * c["g_end"] + dot(
            c["vu"], jnp.concatenate([c["kg"], c["bg"]], axis=0), _TN)


def _wkv_kernel(rf_ref, vf_ref, kkf_ref, lwf_ref, kdf_ref, kbf_ref,
                rb_ref, vb_ref, kkb_ref, lwb_ref, kdb_ref, kbb_ref,
                yf_ref, yb_ref, sf_ref, sb_ref, *, terms):
    @pl.when(pl.program_id(2) == 0)
    def _():
        sf_ref[...] = jnp.zeros_like(sf_ref)
        sb_ref[...] = jnp.zeros_like(sb_ref)

    def f32(ref):
        return ref[...].astype(F32)

    fwd = _wkv_operands(f32(rf_ref), f32(vf_ref), f32(kkf_ref), lwf_ref[...], f32(kdf_ref), f32(kbf_ref), False)
    bwd = _wkv_operands(f32(rb_ref), f32(vb_ref), f32(kkb_ref), lwb_ref[...], f32(kdb_ref), f32(kbb_ref), True)
    _wkv_step([(fwd, sf_ref, yf_ref, False), (bwd, sb_ref, yb_ref, True)], terms)


def _wkv(r, v, kk, lw, kd, kb, n_ctx, terms):
    bsz, t_all, bw = r.shape
    ch = SCAN_CHUNK
    gw = _pick(bw, (SCAN_GROUP_WIDTH, 256, 128))
    nc = t_all // ch
    ncc = n_ctx // ch

    def fwd(c):
        return c

    def bwd(c):
        return jnp.where(c < ncc, ncc - 1 - c, nc + ncc - 1 - c)

    def tok(order):
        return pl.BlockSpec((None, ch, gw), lambda b, g, c: (b, order(c), g))

    def tok2(order, d):
        return pl.BlockSpec((None, None, ch, gw), lambda b, g, c: (d, b, order(c), g))

    out = jax.ShapeDtypeStruct((bsz, t_all, bw), F32)
    state = pltpu.VMEM((gw // LANES, LANES, LANES), F32)
    return pl.pallas_call(
        functools.partial(_wkv_kernel, terms=terms),
        grid=(bsz, bw // gw, nc),
        in_specs=[tok(fwd), tok(fwd), tok(fwd), tok2(fwd, 0), tok2(fwd, 0), tok2(fwd, 0),
                  tok(bwd), tok(bwd), tok(bwd), tok2(bwd, 1), tok2(bwd, 1), tok2(bwd, 1)],
        out_specs=[tok(fwd), tok(bwd)],
        out_shape=[out, out],
        scratch_shapes=[state, state],
        compiler_params=_params("parallel", "parallel", "arbitrary"),
        name="wkv_scan",
    )(r, v, kk, lw, kd, kb, r, v, kk, lw, kd, kb)


def _readout_kernel(yf_ref, yb_ref, bonus_ref, gate_ref, gw_ref, gb_ref, o_ref):
    ones_bd = _head_ones()
    y = yf_ref[...] + yb_ref[...]
    inv = 1.0 / B_HEAD_DIM
    mean = _head_sum(y, ones_bd) * inv
    yc = y - mean
    var = _head_sum(yc * yc, ones_bd) * inv
    out = yc * lax.rsqrt(var + GN_EPS) * gw_ref[...] + gb_ref[...]
    out = out + bonus_ref[...]
    o_ref[...] = (out * _silu(gate_ref[...])).astype(BF16)


def _readout(yf, yb, bonus, z, lay, n_ctx, gn_w, gn_b):
    bsz, t_all, bw = yf.shape
    n_lat = t_all - n_ctx
    tm = 256
    cw = _pick(bw, (512, 256, 128))
    off = n_ctx // tm
    tok = pl.BlockSpec((None, tm, cw), lambda b, j, c: (b, j + off, c))
    chan = pl.BlockSpec((1, cw), lambda b, j, c: (0, c))
    return pl.pallas_call(
        _readout_kernel,
        grid=(bsz, n_lat // tm, bw // cw),
        in_specs=[tok, tok, tok,
                  pl.BlockSpec((None, tm, cw), lambda b, j, c: (b, j + off, lay["gb"] // cw + c)),
                  chan, chan],
        out_specs=pl.BlockSpec((None, tm, cw), lambda b, j, c: (b, j, c)),
        out_shape=jax.ShapeDtypeStruct((bsz, n_lat, bw), BF16),
        compiler_params=_params("parallel", "parallel", "parallel"),
        name="rwkv_readout",
    )(yf, yb, bonus, z, gn_w.reshape(1, bw), gn_b.reshape(1, bw))


def _out_proj_kernel(a_ref, b_ref, wa_ref, wb_ref, x_ref, gate_ref, o_ref):
    y = jnp.dot(a_ref[...], wa_ref[...], preferred_element_type=F32)
    y = y + jnp.dot(b_ref[...], wb_ref[...], preferred_element_type=F32)
    o_ref[...] = x_ref[...] + gate_ref[...] * y


def _out_proj(mix_a, mix_b, w_a, w_b, x, mod):
    bsz, t, d = x.shape
    ka, kb = mix_a.shape[2], mix_b.shape[2]
    tm = _pick(t, (1024, 512, 256))
    tn = _pick(d, (512, 256, 128))
    mod3 = mod.reshape(mod.shape[0], 1, 3 * d)
    return pl.pallas_call(
        _out_proj_kernel,
        grid=(bsz, t // tm, d // tn),
        in_specs=[
            pl.BlockSpec((None, tm, ka), lambda b, i, j: (b, i, 0)),
            pl.BlockSpec((None, tm, kb), lambda b, i, j: (b, i, 0)),
            pl.BlockSpec((ka, tn), lambda b, i, j: (0, j)),
            pl.BlockSpec((kb, tn), lambda b, i, j: (0, j)),
            pl.BlockSpec((None, tm, tn), lambda b, i, j: (b, i, j)),
            pl.BlockSpec((None, 1, tn), lambda b, i, j: (b, 0, 2 * (d // tn) + j)),
        ],
        out_specs=pl.BlockSpec((None, tm, tn), lambda b, i, j: (b, i, j)),
        out_shape=jax.ShapeDtypeStruct((bsz, t, d), F32),
        compiler_params=_params("parallel", "parallel", "parallel"),
        name="out_proj",
    )(mix_a, mix_b, w_a, w_b, x, mod3)


def _fold_lanes(x):
    out = x[:, :LANES]
    for i in range(1, x.shape[1] // LANES):
        out = out + x[:, i * LANES:(i + 1) * LANES]
    return out


def _gm_in_kernel(h_ref, wu_ref, wv_ref, wg_ref, p_ref, gv_ref, s1_ref, s2_ref):
    @pl.when(pl.program_id(1) == 0)
    def _():
        s1_ref[...] = jnp.zeros_like(s1_ref)
        s2_ref[...] = jnp.zeros_like(s2_ref)

    h = h_ref[...]
    gv = jax.nn.gelu(jnp.dot(h, wv_ref[...], preferred_element_type=F32))
    gv_ref[...] = gv
    s1_ref[...] += _fold_lanes(gv)
    s2_ref[...] += _fold_lanes(gv * gv)
    gu = jax.nn.gelu(jnp.dot(h, wu_ref[...], preferred_element_type=F32))
    g = jnp.dot(h, wg_ref[...], preferred_element_type=F32)
    p_ref[...] = (gu * _silu(g)).astype(BF16)


def _gm_in(h, w_in, cwid):
    m, d = h.shape
    tm = _pick(m, (1024, 512, 256))
    tn = _pick(cwid, (512, 256, 128))
    nt = cwid // tn
    stat = jax.ShapeDtypeStruct((m, LANES), F32)
    return pl.pallas_call(
        _gm_in_kernel,
        grid=(m // tm, nt),
        in_specs=[
            pl.BlockSpec((tm, d), lambda i, j: (i, 0), pipeline_mode=pl.Buffered(1)),
            pl.BlockSpec((d, tn), lambda i, j: (0, j)),
            pl.BlockSpec((d, tn), lambda i, j: (0, nt + j)),
            pl.BlockSpec((d, tn), lambda i, j: (0, 2 * nt + j)),
        ],
        out_specs=[
            pl.BlockSpec((tm, tn), lambda i, j: (i, j)),
            pl.BlockSpec((tm, tn), lambda i, j: (i, j)),
            pl.BlockSpec((tm, LANES), lambda i, j: (i, 0)),
            pl.BlockSpec((tm, LANES), lambda i, j: (i, 0)),
        ],
        out_shape=[jax.ShapeDtypeStruct((m, cwid), BF16), jax.ShapeDtypeStruct((m, cwid), F32), stat, stat],
        compiler_params=_params("parallel", "arbitrary"),
        name="gmlp_in",
    )(h, w_in, w_in, w_in)


def _gm_out_kernel(gv_ref, p_ref, s1_ref, s2_ref, lng_ref, lnb_ref, ws_ref, bs_ref, wo_ref, o_ref, y_ref,
                   *, cwid, groups):
    s = pl.program_id(0)
    j = pl.program_id(1)
    tm = gv_ref.shape[0]

    @pl.when(jnp.logical_and(s == 0, j == 0))
    def _():
        y_ref[...] = jnp.zeros_like(y_ref)

    prev = (s + 1) % 2
    acc = jnp.dot(y_ref[prev, 0], wo_ref[0], preferred_element_type=F32)
    for g in range(1, groups):
        acc = acc + jnp.dot(y_ref[prev, g], wo_ref[g], preferred_element_type=F32)
    o_ref[...] = acc

    inv = 1.0 / cwid
    mean = jnp.sum(s1_ref[...], axis=-1, keepdims=True) * inv
    var = jnp.sum(s2_ref[...], axis=-1, keepdims=True) * inv - mean * mean
    rstd = lax.rsqrt(var + LN_EPS)
    vln = ((gv_ref[...] - mean) * rstd * lng_ref[...] + lnb_ref[...]).astype(BF16)
    ws = ws_ref[...].astype(BF16)
    bias = bs_ref[...]
    p = p_ref[...]
    ys = []
    for c in range(tm // C_CHUNK):
        rows = slice(c * C_CHUNK, (c + 1) * C_CHUNK)
        vm = jnp.dot(ws, vln[rows], preferred_element_type=F32) + bias
        ys.append((p[rows].astype(F32) * vm).astype(BF16))
    y_ref[s % 2, j] = jnp.concatenate(ys, axis=0)


def _gm_out(gv, p, s1, s2, ln_g, ln_b, w_s, b_s, w_out):
    m, cwid = gv.shape
    groups = w_s.shape[0]
    gwid = cwid // groups
    d = w_out.shape[1]
    tm = _pick(m, (512, 256))
    nrb = m // tm
    assert d % groups == 0 and (d // groups) % LANES == 0
    tn = d // groups

    def row(s):
        return jnp.minimum(s, nrb - 1)

    def out_idx(s, j):
        return (jnp.maximum(s - 1, 0), jnp.where(s == 0, 0, j))

    return pl.pallas_call(
        functools.partial(_gm_out_kernel, cwid=cwid, groups=groups),
        grid=(nrb + 1, groups),
        in_specs=[
            pl.BlockSpec((tm, gwid), lambda s, j: (row(s), j)),
            pl.BlockSpec((tm, gwid), lambda s, j: (row(s), j)),
            pl.BlockSpec((tm, LANES), lambda s, j: (row(s), 0)),
            pl.BlockSpec((tm, LANES), lambda s, j: (row(s), 0)),
            pl.BlockSpec((1, gwid), lambda s, j: (0, j)),
            pl.BlockSpec((1, gwid), lambda s, j: (0, j)),
            pl.BlockSpec((None, C_CHUNK, C_CHUNK), lambda s, j: (j, 0, 0)),
            pl.BlockSpec((None, C_CHUNK, 1), lambda s, j: (j, 0, 0)),
            pl.BlockSpec((None, groups, gwid, tn), lambda s, j: (j, 0, 0, 0)),
        ],
        out_specs=pl.BlockSpec((tm, tn), out_idx),
        out_shape=jax.ShapeDtypeStruct((m, d), F32),
        scratch_shapes=[pltpu.VMEM((2, groups, tm, gwid), BF16)],
        compiler_params=_params("arbitrary", "arbitrary"),
        name="gmlp_out",
    )(gv, p, s1, s2, ln_g.reshape(1, cwid), ln_b.reshape(1, cwid), w_s,
      b_s.reshape(groups, C_CHUNK, 1),
      w_out.reshape(groups, gwid, d // tn, tn).transpose(2, 0, 1, 3))


def _final_kernel(x_ref, y_ref, gate_ref, g_ref, o_ref):
    x = x_ref[...] + gate_ref[...] * y_ref[...]
    ms = jnp.mean(x * x, axis=-1, keepdims=True)
    o_ref[...] = (x * lax.rsqrt(ms + NORM_EPS)) * g_ref[...]


def _final(x, y, mod, final_g):
    bsz, t, d = x.shape
    tm = 256
    mod3 = mod.reshape(mod.shape[0], 1, 3 * d)
    tok = pl.BlockSpec((None, tm, d), lambda b, j: (b, j, 0))
    return pl.pallas_call(
        _final_kernel,
        grid=(bsz, t // tm),
        in_specs=[tok, tok,
                  pl.BlockSpec((None, 1, d), lambda b, j: (b, 0, 2)),
                  pl.BlockSpec((1, d), lambda b, j: (0, 0))],
        out_specs=tok,
        out_shape=jax.ShapeDtypeStruct((bsz, t, d), F32),
        compiler_params=_params("parallel", "parallel"),
        name="final_norm",
    )(x, y.reshape(bsz, t, d), mod3, final_g.reshape(1, d))


def _ab_layout(d, heads, ab_in):
    aq = heads * A_HEAD_DIM
    bw = d // 2
    akv = (ab_in - 2 * aq - 4 * bw - 4 * LORA_RANK) // 2
    lay = {"aq": aq, "akv": akv, "bw": bw}
    off = 0
    for name, width in (("q", aq), ("ga", aq), ("r", bw), ("kb", bw), ("vb", bw), ("gb", bw),
                        ("lora", 2 * LORA_SLOT), ("k", akv), ("v", akv)):
        lay[name] = off
        off += width
    lay["width"] = off
    return lay


def _ab_weight(w_in, lay):
    aq, akv, bw = lay["aq"], lay["akv"], lay["bw"]
    widths = (aq, akv, akv, aq, bw, bw, bw, bw, 2 * LORA_RANK, 2 * LORA_RANK)
    parts = []
    off = 0
    for w in widths:
        parts.append(w_in[:, off:off + w])
        off += w
    q, k, v, ga, r, kb, vb, gb, lw, la = parts
    q = _rope_head_order(q, aq // A_HEAD_DIM)
    k = _rope_head_order(k, akv // A_HEAD_DIM)
    pad =jnp.zeros((w_in.shape[0], LORA_SLOT - 2 * LORA_RANK), w_in.dtype)
    return jnp.concatenate([q, ga, r, kb, vb, gb, lw, pad, la, pad, k, v], axis=1).astype(BF16)


def kernel(x, c, ctx, c_ctx, mod_w, mod_b, norm_g, ab_w_in, ab_w_out, attn_sink, rwkv_conv, rwkv_w0, rwkv_w2,
           rwkv_a0, rwkv_a2, rwkv_k_k, rwkv_k_a, rwkv_r_k, rwkv_gn_w, rwkv_gn_b, gm_w_in, gm_ln_g, gm_ln_b,
           gm_w_s, gm_b_s, gm_w_out, final_g):
    bsz, n_lat, d = x.shape
    n_ctx = ctx.shape[1]
    heads = attn_sink.shape[1]

    cvec = jnp.concatenate([c, c_ctx[None, :], jnp.zeros((SUBLANES - bsz - 1, d), F32)], axis=0)
    mod = _modulation(cvec, mod_w, mod_b)

    lay = _ab_layout(d, heads, ab_w_in.shape[2])
    w_in0 = _ab_weight(ab_w_in[0], lay)
    h0 = _norm_mod(x, ctx, norm_g[0], mod[0], bsz)
    t_all = n_ctx + n_lat
    z = _matmul(h0.reshape(bsz * t_all, d), w_in0).reshape(bsz, t_all, lay["width"])
    mix_a = _attention(z, attn_sink[0], _rope_tables(n_lat), lay, n_ctx, n_lat)
    r, v, kk, bonus, lw, kd, kb = _rwkv_prep(z, lay, n_ctx, rwkv_conv[0], rwkv_w0[0], rwkv_w2[0], rwkv_a0[0],
                                             rwkv_a2[0], rwkv_k_k[0], rwkv_k_a[0], rwkv_r_k[0])
    yf, yb = _wkv(r, v, kk, lw, kd, kb, n_ctx, terms=1)
    mix_b = _readout(yf, yb, bonus, z, lay, n_ctx, rwkv_gn_w[0], rwkv_gn_b[0])
    w_out0 = ab_w_out[0].astype(BF16)
    x1 = _out_proj(mix_a, mix_b, w_out0[:lay["aq"]], w_out0[lay["aq"]:], x, mod[0])

    cwid = gm_ln_g.shape[1]
    h1 = _norm_mod(x1, None, norm_g[1], mod[1], 0)
    p, gv, s1, s2 = _gm_in(h1.reshape(bsz * n_lat, d), gm_w_in[0].astype(BF16), cwid)
    y1 = _gm_out(gv, p, s1, s2, gm_ln_g[0], gm_ln_b[0], gm_w_s[0], gm_b_s[0], gm_w_out[0].astype(BF16))
    return _final(x1, y1, mod[1], final_g)
```

```python
import functools
import math

import jax
import jax.numpy as jnp
from jax import lax
from jax.experimental import pallas as pl
from jax.experimental.pallas import tpu as pltpu

F32 = jnp.float32
BF16 = jnp.bfloat16

A_HEAD_DIM = 128
A_WINDOW = 128
A_BLOCK = 128
GRID_W = 64
ROPE_THETA = 10000.0
B_HEAD_DIM = 64
LORA_RANK = 96
GN_EPS = 64e-5
C_CHUNK = 128
LN_EPS = 1e-5
NORM_EPS = 1e-6

LANES = 128
SUBLANES = 8
VMEM_LIMIT_BYTES = 56 * 1024 * 1024

SCAN_CHUNK = 64
SCAN_LEVELS = 6
SCAN_GROUP_WIDTH = 2048
LORA_SLOT = 256
LORA_LA_START = 128
MASK_VALUE = -1e30
DECAY_SCALE = math.exp(-0.5)


def _params(*semantics):
    return pltpu.CompilerParams(dimension_semantics=semantics, vmem_limit_bytes=VMEM_LIMIT_BYTES)


def _pick(n, candidates):
    for c in candidates:
        if n % c == 0:
            return c
    raise ValueError(f"no tile in {candidates} divides {n}")


def _split(x, terms):
    parts = []
    rest = x
    for i in range(terms):
        p = rest.astype(BF16)
        parts.append(p)
        if i + 1 < terms:
            rest = rest - p.astype(F32)
    return parts


_NN = (((1,), (0,)), ((), ()))
_NT = (((1,), (1,)), ((), ()))
_TN = (((0,), (0,)), ((), ()))


def _dotp(a, b, dims=_NN, ta=1, tb=1):
    pa = _split(a, ta)
    pb = _split(b, tb)
    out = None
    for i in range(ta):
        for j in range(tb):
            if i + j >= max(ta, tb):
                continue
            t = lax.dot_general(pa[i], pb[j], dims, preferred_element_type=F32)
            out = t if out is None else out + t
    return out


def _silu(x):
    return x * jax.nn.sigmoid(x)


def _mod_kernel(c_ref, w_ref, b_ref, o_ref):
    s = _silu(c_ref[...])
    o_ref[...] = _dotp(s, w_ref[...], ta=2, tb=2) + b_ref[...]


def _modulation(cvec, mod_w, mod_b):
    depth, d, n = mod_w.shape
    rows = cvec.shape[0]
    tn = _pick(n, (512, 256, 128))
    return pl.pallas_call(
        _mod_kernel,
        grid=(depth, n // tn),
        in_specs=[
            pl.BlockSpec((rows, d), lambda l, j: (0, 0)),
            pl.BlockSpec((None, d, tn), lambda l, j: (l, 0, j)),
            pl.BlockSpec((None, 1, tn), lambda l, j: (l, 0, j)),
        ],
        out_specs=pl.BlockSpec((None, rows, tn), lambda l, j: (l, 0, j)),
        out_shape=jax.ShapeDtypeStruct((depth, rows, n), F32),
        compiler_params=_params("parallel", "parallel"),
        name="modulation",
    )(cvec, mod_w, mod_b.reshape(depth, 1, n))


def _norm_rows(x, g, shift, scale):
    ms = jnp.mean(x * x, axis=-1, keepdims=True)
    y = x * lax.rsqrt(ms + NORM_EPS)
    return (y * g) * (1.0 + scale) + shift


def _norm_mod_kernel(x_ref, c_ref, g_ref, shift_ref, scale_ref, o_ref, *, ctx_blocks):
    j = pl.program_id(1)

    @pl.when(j < ctx_blocks)
    def _():
        o_ref[...] = _norm_rows(c_ref[...], g_ref[...], shift_ref[...], scale_ref[...]).astype(BF16)

    @pl.when(j >= ctx_blocks)
    def _():
        o_ref[...] = _norm_rows(x_ref[...], g_ref[...], shift_ref[...], scale_ref[...]).astype(BF16)


def _norm_mod(x, ctx, g, mod, ctx_row):
    bsz, t, d = x.shape
    tm = 256
    if ctx is None:
        ctx = x
        cb = 0
    else:
        cb = ctx.shape[1] // tm
    nblk = cb + t // tm
    rows = mod.shape[0]
    mod3 = mod.reshape(rows, 1, 3 * d)

    def mrow(b, j):
        return jnp.where(j < cb, ctx_row, b)

    return pl.pallas_call(
        functools.partial(_norm_mod_kernel, ctx_blocks=cb),
        grid=(bsz, nblk),
        in_specs=[
            pl.BlockSpec((None, tm, d), lambda b, j: (b, jnp.maximum(j - cb, 0), 0)),
            pl.BlockSpec((None, tm, d), lambda b, j: (b, jnp.minimum(j, max(cb - 1, 0)), 0)),
            pl.BlockSpec((1, d), lambda b, j: (0, 0)),
            pl.BlockSpec((None, 1, d), lambda b, j: (mrow(b, j), 0, 0)),
            pl.BlockSpec((None, 1, d), lambda b, j: (mrow(b, j), 0, 1)),
        ],
        out_specs=pl.BlockSpec((None, tm, d), lambda b, j: (b, j, 0)),
        out_shape=jax.ShapeDtypeStruct((bsz, nblk * tm, d), BF16),
        compiler_params=_params("parallel", "parallel"),
        name="norm_mod",
    )(x, ctx, g.reshape(1, d), mod3, mod3)


def _mm_kernel(a_ref, b_ref, o_ref):
    o_ref[...] = jnp.dot(a_ref[...], b_ref[...], preferred_element_type=F32)


def _matmul(a, b):
    m, k = a.shape
    n = b.shape[1]
    tm = _pick(m, (1536, 1024, 768, 512, 256))
    tn = _pick(n, (512, 256, 128))
    return pl.pallas_call(
        _mm_kernel,
        grid=(m // tm, n // tn),
        in_specs=[
            pl.BlockSpec((tm, k), lambda i, j: (i, 0)),
            pl.BlockSpec((k, tn), lambda i, j: (0, j)),
        ],
        out_specs=pl.BlockSpec((tm, tn), lambda i, j: (i, j)),
        out_shape=jax.ShapeDtypeStruct((m, n), F32),
        compiler_params=_params("parallel", "parallel"),
        name="in_proj",
    )(a, b)


def _rope(x, cos, sin_signed):
    out = []
    for h in range(x.shape[1] // A_HEAD_DIM):
        xh = x[:, h * A_HEAD_DIM:(h + 1) * A_HEAD_DIM]
        out.append(xh * cos + pltpu.roll(xh, A_HEAD_DIM // 2, 1) * sin_signed)
    return out


def _rope_head_order(w, n_heads):
    w5 = w.reshape(w.shape[0], n_heads, 2, 2, A_HEAD_DIM // 4)
    return jnp.swapaxes(w5, 2, 3).reshape(w.shape)


def _attn_kernel(sink_ref, q_ref, g_ref, kp_ref, kc_ref, kn_ref, vp_ref, vc_ref, vn_ref,
                 kx_ref, vx_ref, cq_ref, sq_ref, cp_ref, sp_ref, cn_ref, sn_ref, o_ref,
                 *, n_blocks, kv_heads, group):
    n = pl.program_id(1)
    blk = A_BLOCK
    lctx = kx_ref.shape[0]
    scale = A_HEAD_DIM ** -0.5

    q_heads = _rope(q_ref[...], cq_ref[...], sq_ref[...])
    kp = _rope(kp_ref[...], cp_ref[...], sp_ref[...])
    kc = _rope(kc_ref[...], cq_ref[...], sq_ref[...])
    kn = _rope(kn_ref[...], cn_ref[...], sn_ref[...])

    qi = lax.broadcasted_iota(jnp.int32, (blk, lctx + 3 * blk), 0)
    kj = lax.broadcasted_iota(jnp.int32, (blk, lctx + 3 * blk), 1) - lctx
    no_prev = jnp.where(n > 0, 0, blk)
    no_next = jnp.where(n < n_blocks - 1, 0, blk)
    prev_ok = (kj >= qi + no_prev) & (kj < blk)
    next_ok = (kj - 2 * blk <= qi - no_next) & (kj >= 2 * blk)
    valid = (kj < 0) | prev_ok | ((kj >= blk) & (kj < 2 * blk)) | next_ok
    mask = jnp.where(valid, 0.0, MASK_VALUE).astype(F32)
    mask = jnp.concatenate([mask] * group, axis=0)

    gate = g_ref[...]
    scores, sinks, probs, dens = [], [], [], []
    for hk in range(kv_heads):
        sl = slice(hk * A_HEAD_DIM, (hk + 1) * A_HEAD_DIM)
        kcat = jnp.concatenate([kx_ref[:, sl], kp[hk], kc[hk], kn[hk]], axis=0).astype(BF16)
        qs = jnp.concatenate([q_heads[hk * group + g] * scale for g in range(group)], axis=0).astype(BF16)
        scores.append(lax.dot_general(qs, kcat, _NT, preferred_element_type=F32) + mask)
        sinks.append(jnp.concatenate(
            [jnp.full((blk, 1), sink_ref[hk * group + g], F32) for g in range(group)], axis=0))
    for hk in range(kv_heads):
        m = jnp.maximum(jnp.max(scores[hk], axis=-1, keepdims=True), sinks[hk])
        p = jnp.exp(scores[hk] - m)
        dens.append(jnp.exp(sinks[hk] - m) + jnp.sum(p, axis=-1, keepdims=True))
        probs.append(p.astype(BF16))
    for hk in range(kv_heads):
        sl = slice(hk * A_HEAD_DIM, (hk + 1) * A_HEAD_DIM)
        vcat = jnp.concatenate([vx_ref[:, sl], vp_ref[:, sl], vc_ref[:, sl], vn_ref[:, sl]], axis=0).astype(BF16)
        o = jnp.dot(probs[hk], vcat, preferred_element_type=F32) / dens[hk]
        for g in range(group):
            h = hk * group + g
            hs = slice(h * A_HEAD_DIM, (h + 1) * A_HEAD_DIM)
            o_ref[:, hs] = (o[g * blk:(g + 1) * blk] * _silu(gate[:, hs])).astype(BF16)


def _attention(za, zb, sink, tables, lay, n_ctx, n_lat):
    bsz = za.shape[0]
    aq, akv = lay["aq"], lay["akv"]
    heads = aq // A_HEAD_DIM
    kv_heads = akv // A_HEAD_DIM
    blk = A_BLOCK
    nb = n_lat // blk
    off = n_ctx // blk
    kcol = lay["k"] // akv
    vcol = lay["v"] // akv
    cos, sin = tables

    def prev(n):
        return jnp.maximum(n - 1, 0)

    def nxt(n):
        return jnp.minimum(n + 1, nb - 1)

    def zspec(width, rowf, col):
        return pl.BlockSpec((None, blk, width), lambda b, n: (b, rowf(n) + off, col))

    def tspec(rowf):
        return pl.BlockSpec((blk, A_HEAD_DIM), lambda b, n: (rowf(n), 0))

    ident = lambda n: n
    return pl.pallas_call(
        functools.partial(_attn_kernel, n_blocks=nb, kv_heads=kv_heads, group=heads // kv_heads),
        grid=(bsz, nb),
        in_specs=[
            pl.BlockSpec(memory_space=pltpu.SMEM),
            zspec(aq, ident, lay["q"] // aq),
            zspec(aq, ident, lay["ga"] // aq),
            zspec(akv, prev, kcol), zspec(akv, ident, kcol), zspec(akv, nxt, kcol),
            zspec(akv, prev, vcol), zspec(akv, ident, vcol), zspec(akv, nxt, vcol),
            pl.BlockSpec((None, n_ctx, akv), lambda b, n: (b, 0, kcol)),
            pl.BlockSpec((None, n_ctx, akv), lambda b, n: (b, 0, vcol)),
            tspec(ident), tspec(ident), tspec(prev), tspec(prev), tspec(nxt), tspec(nxt),
        ],
        out_specs=pl.BlockSpec((None, blk, aq), lambda b, n: (b, n, 0)),
        out_shape=jax.ShapeDtypeStruct((bsz, n_lat, aq), BF16),
        compiler_params=_params("parallel", "parallel"),
        name="band_attention",
    )(sink, za, zb, za, za, za, za, za, za, za, za, cos, sin, cos, sin, cos, sin)


def _rope_tables(n_tok):
    rows = n_tok // GRID_W
    row = jnp.repeat(jnp.arange(rows, dtype=F32), GRID_W)
    col = jnp.tile(jnp.arange(GRID_W, dtype=F32), rows)
    n_freq = A_HEAD_DIM // 4
    inv_freq = ROPE_THETA ** (-jnp.arange(n_freq, dtype=F32) / n_freq)
    ar = row[:, None] * inv_freq
    ac = col[:, None] * inv_freq
    cos = jnp.concatenate([jnp.cos(ar), jnp.cos(ac), jnp.cos(ar), jnp.cos(ac)], axis=-1)
    sin = jnp.concatenate([-jnp.sin(ar), -jnp.sin(ac), jnp.sin(ar), jnp.sin(ac)], axis=-1)
    return cos, sin


def _head_sum(x, ones_bd):
    out = []
    for i in range(x.shape[1] // LANES):
        out.append(_dotp(x[:, i * LANES:(i + 1) * LANES], ones_bd, ta=2, tb=1))
    return jnp.concatenate(out, axis=1) if len(out) > 1 else out[0]


def _head_ones():
    r = lax.broadcasted_iota(jnp.int32, (LANES, LANES), 0) // B_HEAD_DIM
    c = lax.broadcasted_iota(jnp.int32, (LANES, LANES), 1) // B_HEAD_DIM
    return (r == c).astype(F32)


def _prep_kernel(zr_ref, zk_ref, zv_ref, zrp_ref, zkp_ref, zvp_ref, zrn_ref, zkn_ref, zvn_ref,
                 zwa_ref, cwr_ref, cwk_ref, cwv_ref, w2_ref, a2_ref,
                 w0_ref, a0_ref, kk_ref, ka_ref, rk_ref,
                 r_o, v_o, kk_o, bonus_o, lw_o, kd_o, kb_o, *, ctx_blocks, n_blocks):
    j = pl.program_id(1)
    tm = zr_ref.shape[0]
    has_prev = jnp.logical_and(j != 0, j != ctx_blocks)
    has_next = jnp.logical_and(j != ctx_blocks - 1, j != n_blocks - 1)
    rows = lax.broadcasted_iota(jnp.int32, zr_ref.shape, 0)

    def conv(x_ref, p_ref, n_ref, w_ref):
        x = x_ref[...]
        w = w_ref[...]
        before = jnp.where(has_prev, p_ref[SUBLANES - 1:SUBLANES, :], 0.0)
        after = jnp.where(has_next, n_ref[0:1, :], 0.0)
        xm = jnp.where(rows == 0, before, pltpu.roll(x, 1, 0))
        xp = jnp.where(rows == tm - 1, after, pltpu.roll(x, tm - 1, 0))
        return xm * w[0:1] + x * w[1:2] + xp * w[2:3]

    r = conv(zr_ref, zrp_ref, zrn_ref, cwr_ref)
    k = conv(zk_ref, zkp_ref, zkn_ref, cwk_ref)
    v = conv(zv_ref, zvp_ref, zvn_ref, cwv_ref)
    ones_bd = _head_ones()

    kx = k * kk_ref[...]
    kkn = kx / jnp.maximum(jnp.sqrt(_head_sum(kx * kx, ones_bd)), 1e-12)
    r_o[...] = r.astype(BF16)
    v_o[...] = v.astype(BF16)
    kk_o[...] = kkn.astype(BF16)
    bonus_o[...] = _head_sum(r * k * rk_ref[...], ones_bd) * v

    zwa = zwa_ref[...]
    lw = _split(jnp.tanh(zwa[:, :LORA_SLOT]), 2)
    la = _split(zwa[:, LORA_LA_START:LORA_LA_START + LORA_SLOT], 2)
    ka = ka_ref[...]

    def lora(x, w_ref, d):
        out = jnp.dot(x[0], w_ref[d, 0], preferred_element_type=F32)
        out = out + jnp.dot(x[0], w_ref[d, 1], preferred_element_type=F32)
        return out + jnp.dot(x[1], w_ref[d, 0], preferred_element_type=F32)

    for d in range(2):
        xw = w0_ref[d:d + 1, :] + lora(lw, w2_ref, d)
        lw_o[d] = -DECAY_SCALE * jax.nn.sigmoid(xw)
        a = jax.nn.sigmoid(a0_ref[d:d + 1, :] + lora(la, a2_ref, d))
        kd_o[d] = (k * (1.0 + (a - 1.0) * ka)).astype(BF16)
        kb_o[d] = (kkn * a).astype(BF16)


def _rwkv_prep(z, lay, n_ctx, conv_w, w0, w2, a0, a2, k_k, k_a, r_k):
    bsz, t_all, _ = z.shape
    bw = lay["bw"]
    tm = 256
    cw = _pick(bw, (512, 256, 128))
    nblk = t_all // tm
    cb = n_ctx // tm
    sub = tm // SUBLANES
    last8 = t_all // SUBLANES - 1

    def main(off):
        return pl.BlockSpec((None, tm, cw), lambda b, j, c: (b, j, off // cw + c))

    def before(off):
        return pl.BlockSpec((None, SUBLANES, cw), lambda b, j, c: (b, jnp.maximum(j * sub - 1, 0), off // cw + c))

    def after(off):
        return pl.BlockSpec((None, SUBLANES, cw), lambda b, j, c: (b, jnp.minimum((j + 1) * sub, last8), off // cw + c))

    def chan(rows, off=0):
        return pl.BlockSpec((rows, cw), lambda b, j, c: (0, off // cw + c))

    def pad_lora(w, first_row):
        full = jnp.zeros((2, LORA_SLOT, bw), F32)
        for d in range(2):
            full = full.at[d, first_row + d * LORA_RANK:first_row + (d + 1) * LORA_RANK].set(w[d])
        hi = full.astype(BF16)
        lo = (full - hi.astype(F32)).astype(BF16)
        return jnp.stack([hi, lo], axis=1)

    lora_spec = pl.BlockSpec((2, 2, LORA_SLOT, cw), lambda b, j, c: (0, 0, 0, c))
    tok = pl.BlockSpec((None, tm, cw), lambda b, j, c: (b, j, c))
    tok2 = pl.BlockSpec((2, None, tm, cw), lambda b, j, c: (0, b, j, c))
    one = jax.ShapeDtypeStruct((bsz, t_all, bw), BF16)
    one32 = jax.ShapeDtypeStruct((bsz, t_all, bw), F32)
    two = jax.ShapeDtypeStruct((2, bsz, t_all, bw), BF16)
    two32 = jax.ShapeDtypeStruct((2, bsz, t_all, bw), F32)
    return pl.pallas_call(
        functools.partial(_prep_kernel, ctx_blocks=cb, n_blocks=nblk),
        grid=(bsz, nblk, bw // cw),
        in_specs=[
            main(lay["r"]), main(lay["kb"]), main(lay["vb"]),
            before(lay["r"]), before(lay["kb"]), before(lay["vb"]),
            after(lay["r"]), after(lay["kb"]), after(lay["vb"]),
            pl.BlockSpec((None, tm, 2 * LORA_SLOT), lambda b, j, c: (b, j, lay["lora"] // (2 * LORA_SLOT))),
            chan(3, 0), chan(3, bw), chan(3, 2 * bw),
            lora_spec, lora_spec,
            chan(2), chan(2), chan(1), chan(1), chan(1),
        ],
        out_specs=[tok, tok, tok, tok, tok2, tok2, tok2],
        out_shape=[one, one, one, one32, two32, two, two],
        compiler_params=_params("parallel", "parallel", "parallel"),
        name="rwkv_prep",
    )(z, z, z, z, z, z, z, z, z, z, conv_w, conv_w, conv_w,
      pad_lora(w2, 0), pad_lora(a2, 2 * LORA_RANK - LORA_LA_START),
      w0, a0, k_k.reshape(1, bw), k_a.reshape(1, bw), r_k.reshape(1, bw))


def _stack_heads(x, lane_is_first):
    return jnp.concatenate([jnp.where(lane_is_first, x, 0.0), jnp.where(lane_is_first, 0.0, x)], axis=0)


def _wkv_operands(r, v, kk, lw, kd, kb, reverse):
    ch = SCAN_CHUNK
    row = lax.broadcasted_iota(jnp.int32, (ch, ch), 0)
    col = lax.broadcasted_iota(jnp.int32, (ch, ch), 1)
    incl = (row <= col) if reverse else (row >= col)
    cum = _dotp(incl.astype(F32), lw, ta=1, tb=3)
    last = cum[0:1, :] if reverse else cum[ch - 1:ch, :]
    g_in = jnp.exp(cum)
    g_inv = jnp.exp(-cum)
    g_ex = jnp.exp(cum - lw)
    g_rem = jnp.exp(last - cum)
    g_end = jnp.exp(last)
    a_t = -kk * g_ex
    b_t = kb * g_inv
    k_t = kd * g_inv
    r_t = r * g_in
    k_g = kd * g_rem
    b_g = kb * g_rem
    return dict(a=a_t, r=r_t, b=b_t, k=k_t, v=v, kg=k_g, bg=b_g, g_end=g_end)


def _wkv_step(streams, terms):
    ch = SCAN_CHUNK
    two = 2 * ch
    dot = functools.partial(_dotp, ta=terms, tb=terms)
    i2 = lax.broadcasted_iota(jnp.int32, (two, two), 0) % ch
    j2 = lax.broadcasted_iota(jnp.int32, (two, two), 1) % ch
    first = lax.broadcasted_iota(jnp.int32, (ch, LANES), 1) < B_HEAD_DIM
    st = functools.partial(_stack_heads, lane_is_first=first)

    chains = []
    for ops, s_ref, y_ref, reverse in streams:
        m_incl = (i2 <= j2) if reverse else (i2 >= j2)
        m_strict = (i2 < j2) if reverse else (i2 > j2)
        for p in range(ops["a"].shape[1] // LANES):
            sl = slice(p * LANES, (p + 1) * LANES)
            c = {n: st(ops[n][:, sl]) for n in ("a", "r", "b", "k", "v", "kg", "bg")}
            c.update(sl=sl, p=p, s_ref=s_ref, y_ref=y_ref, m_incl=m_incl, m_strict=m_strict,
                     g_end=ops["g_end"][:, sl])
            chains.append(c)

    four = 2 * two
    for c in chains:
        c["state"] = c["s_ref"][c["p"]]
        lhs = jnp.concatenate([c["a"], c["r"]], axis=0)
        res = dot(lhs, jnp.concatenate([c["b"], c["k"], c["state"]], axis=0), _NT)
        c["dmat"] = res[:, :four]
        c["from_state"] = res[:, four:]
    for c in chains:
        d = c["dmat"]
        c["d_ab"] = jnp.where(c["m_strict"], d[:two, :two], 0.0)
        c["d_rb"] = jnp.where(c["m_incl"], d[two:, :two], 0.0)
        c["d_rk"] = jnp.where(c["m_incl"], d[two:, two:], 0.0)
        c["x"] = c["from_state"][:two] + dot(jnp.where(c["m_strict"], d[:two, two:], 0.0), c["v"])
    for lvl in range(SCAN_LEVELS):
        for c in chains:
            if lvl + 1 < SCAN_LEVELS:
                res = dot(c["d_ab"], jnp.concatenate([c["d_ab"], c["x"]], axis=1))
                c["d_ab"] = res[:, :two]
                c["x"] = c["x"] + res[:, two:]
            else:
                c["x"] = c["x"] + dot(c["d_ab"], c["x"])
    for c in chains:
        c["vu"] = jnp.concatenate([c["v"], c["x"]], axis=0)
        y_s = c["from_state"][two:] + dot(jnp.concatenate([c["d_rk"], c["d_rb"]], axis=1), c["vu"])
        c["y_ref"][:, c["sl"]] = y_s[:ch] + y_s[ch:]
    for c in chains:
        c["s_ref"][c["p"]] = c["state"] * c["g_end"] + dot(
            c["vu"], jnp.concatenate([c["kg"], c["bg"]], axis=0), _TN)


def _wkv_kernel(rf_ref, vf_ref, kkf_ref, lwf_ref, kdf_ref, kbf_ref,
                rb_ref, vb_ref, kkb_ref, lwb_ref, kdb_ref, kbb_ref,
                yf_ref, yb_ref, sf_ref, sb_ref, *, terms):
    @pl.when(pl.program_id(2) == 0)
    def _():
        sf_ref[...] = jnp.zeros_like(sf_ref)
        sb_ref[...] = jnp.zeros_like(sb_ref)

    def f32(ref):
        return ref[...].astype(F32)

    fwd = _wkv_operands(f32(rf_ref), f32(vf_ref), f32(kkf_ref), lwf_ref[...], f32(kdf_ref), f32(kbf_ref), False)
    bwd = _wkv_operands(f32(rb_ref), f32(vb_ref), f32(kkb_ref), lwb_ref[...], f32(kdb_ref), f32(kbb_ref), True)
    _wkv_step([(fwd, sf_ref, yf_ref, False), (bwd, sb_ref, yb_ref, True)], terms)


def _wkv(r, v, kk, lw, kd, kb, n_ctx, terms):
    bsz, t_all, bw = r.shape
    ch = SCAN_CHUNK
    gw = _pick(bw, (SCAN_GROUP_WIDTH, 256, 128))
    nc = t_all // ch
    ncc = n_ctx // ch

    def fwd(c):
        return c

    def bwd(c):
        return jnp.where(c < ncc, ncc - 1 - c, nc + ncc - 1 - c)

    def tok(order):
        return pl.BlockSpec((None, ch, gw), lambda b, g, c: (b, order(c), g))

    def tok2(order, d):
        return pl.BlockSpec((None, None, ch, gw), lambda b, g, c: (d, b, order(c), g))

    out = jax.ShapeDtypeStruct((bsz, t_all, bw), F32)
    state = pltpu.VMEM((gw // LANES, LANES, LANES), F32)
    return pl.pallas_call(
        functools.partial(_wkv_kernel, terms=terms),
        grid=(bsz, bw // gw, nc),
        in_specs=[tok(fwd), tok(fwd), tok(fwd), tok2(fwd, 0), tok2(fwd, 0), tok2(fwd, 0),
                  tok(bwd), tok(bwd), tok(bwd), tok2(bwd, 1), tok2(bwd, 1), tok2(bwd, 1)],
        out_specs=[tok(fwd), tok(bwd)],
        out_shape=[out, out],
        scratch_shapes=[state, state],
        compiler_params=_params("parallel", "parallel", "arbitrary"),
        name="wkv_scan",
    )(r, v, kk, lw, kd, kb, r, v, kk, lw, kd, kb)


def _readout_kernel(yf_ref, yb_ref, bonus_ref, gate_ref, gw_ref, gb_ref, o_ref):
    ones_bd = _head_ones()
    y = yf_ref[...] + yb_ref[...]
    inv = 1.0 / B_HEAD_DIM
    mean = _head_sum(y, ones_bd) * inv
    yc = y - mean
    var = _head_sum(yc * yc, ones_bd) * inv
    out = yc * lax.rsqrt(var + GN_EPS) * gw_ref[...] + gb_ref[...]
    out = out + bonus_ref[...]
    o_ref[...] = (out * _silu(gate_ref[...])).astype(BF16)


def _readout(yf, yb, bonus, z, lay, n_ctx, gn_w, gn_b):
    bsz, t_all, bw = yf.shape
    n_lat = t_all - n_ctx
    tm = 256
    cw = _pick(bw, (2048, 512, 256, 128))
    off = n_ctx // tm
    tok = pl.BlockSpec((None, tm, cw), lambda b, j, c: (b, j + off, c))
    chan = pl.BlockSpec((1, cw), lambda b, j, c: (0, c))
    return pl.pallas_call(
        _readout_kernel,
        grid=(bsz, n_lat // tm, bw // cw),
        in_specs=[tok, tok, tok,
                  pl.BlockSpec((None, tm, cw), lambda b, j, c: (b, j + off, lay["gb"] // cw + c)),
                  chan, chan],
        out_specs=pl.BlockSpec((None, tm, cw), lambda b, j, c: (b, j, c)),
        out_shape=jax.ShapeDtypeStruct((bsz, n_lat, bw), BF16),
        compiler_params=_params("parallel", "parallel", "parallel"),
        name="rwkv_readout",
    )(yf, yb, bonus, z, gn_w.reshape(1, bw), gn_b.reshape(1, bw))


def _out_proj_kernel(a_ref, b_ref, wa_ref, wb_ref, x_ref, gate_ref, o_ref):
    y = jnp.dot(a_ref[...], wa_ref[...], preferred_element_type=F32)
    y = y + jnp.dot(b_ref[...], wb_ref[...], preferred_element_type=F32)
    o_ref[...] = x_ref[...] + gate_ref[...] * y


def _out_proj(mix_a, mix_b, w_out, x, mod):
    bsz, t, d = x.shape
    ka, kb = mix_a.shape[2], mix_b.shape[2]
    assert ka % kb == 0 and w_out.shape[0] == ka + kb
    tm = _pick(t, (1024, 512, 256))
    tn = _pick(d, (512, 256, 128))
    mod3 = mod.reshape(mod.shape[0], 1, 3 * d)
    return pl.pallas_call(
        _out_proj_kernel,
        grid=(bsz, t // tm, d // tn),
        in_specs=[
            pl.BlockSpec((None, tm, ka), lambda b, i, j: (b, i, 0)),
            pl.BlockSpec((None, tm, kb), lambda b, i, j: (b, i, 0)),
            pl.BlockSpec((ka, tn), lambda b, i, j: (0, j)),
            pl.BlockSpec((kb, tn), lambda b, i, j: (ka // kb, j)),
            pl.BlockSpec((None, tm, tn), lambda b, i, j: (b, i, j)),
            pl.BlockSpec((None, 1, tn), lambda b, i, j: (b, 0, 2 * (d // tn) + j)),
        ],
        out_specs=pl.BlockSpec((None, tm, tn), lambda b, i, j: (b, i, j)),
        out_shape=jax.ShapeDtypeStruct((bsz, t, d), F32),
        compiler_params=_params("parallel", "parallel", "parallel"),
        name="out_proj",
    )(mix_a, mix_b, w_out, w_out, x, mod3)


def _fold_lanes(x):
    out = x[:, :LANES]
    for i in range(1, x.shape[1] // LANES):
        out = out + x[:, i * LANES:(i + 1) * LANES]
    return out


def _gm_in_kernel(h_ref, wu_ref, wv_ref, wg_ref, p_ref, gv_ref, s1_ref, s2_ref):
    @pl.when(pl.program_id(1) == 0)
    def _():
        s1_ref[...] = jnp.zeros_like(s1_ref)
        s2_ref[...] = jnp.zeros_like(s2_ref)

    h = h_ref[...]
    gv = jax.nn.gelu(jnp.dot(h, wv_ref[...], preferred_element_type=F32))
    gv_ref[...] = gv
    s1_ref[...] += _fold_lanes(gv)
    s2_ref[...] += _fold_lanes(gv * gv)
    gu = jax.nn.gelu(jnp.dot(h, wu_ref[...], preferred_element_type=F32))
    g = jnp.dot(h, wg_ref[...], preferred_element_type=F32)
    p_ref[...] = (gu * _silu(g)).astype(BF16)


def _gm_in(h, w_in, cwid):
    m, d = h.shape
    tm = _pick(m, (1024, 512, 256))
    tn = _pick(cwid, (512, 256, 128))
    nt = cwid // tn
    stat = jax.ShapeDtypeStruct((m, LANES), F32)
    return pl.pallas_call(
        _gm_in_kernel,
        grid=(m // tm, nt),
        in_specs=[
            pl.BlockSpec((tm, d), lambda i, j: (i, 0), pipeline_mode=pl.Buffered(1)),
            pl.BlockSpec((d, tn), lambda i, j: (0, j)),
            pl.BlockSpec((d, tn), lambda i, j: (0, nt + j)),
            pl.BlockSpec((d, tn), lambda i, j: (0, 2 * nt + j)),
        ],
        out_specs=[
            pl.BlockSpec((tm, tn), lambda i, j: (i, j)),
            pl.BlockSpec((tm, tn), lambda i, j: (i, j)),
            pl.BlockSpec((tm, LANES), lambda i, j: (i, 0)),
            pl.BlockSpec((tm, LANES), lambda i, j: (i, 0)),
        ],
        out_shape=[jax.ShapeDtypeStruct((m, cwid), BF16), jax.ShapeDtypeStruct((m, cwid), F32), stat, stat],
        compiler_params=_params("parallel", "arbitrary"),
        name="gmlp_in",
    )(h, w_in, w_in, w_in)


def _gm_out_kernel(gv_ref, p_ref, s1_ref, s2_ref, lng_ref, lnb_ref, ws_ref, bs_ref, wo_ref, o_ref, y_ref,
                   *, cwid, groups):
    s = pl.program_id(0)
    j = pl.program_id(1)
    tm = gv_ref.shape[0]

    @pl.when(jnp.logical_and(s == 0, j == 0))
    def _():
        y_ref[...] = jnp.zeros_like(y_ref)

    prev = (s + 1) % 2
    acc = jnp.dot(y_ref[prev, 0], wo_ref[0], preferred_element_type=F32)
    for g in range(1, groups):
        acc = acc + jnp.dot(y_ref[prev, g], wo_ref[g], preferred_element_type=F32)
    o_ref[...] = acc

    inv = 1.0 / cwid
    mean = jnp.sum(s1_ref[...], axis=-1, keepdims=True) * inv
    var = jnp.sum(s2_ref[...], axis=-1, keepdims=True) * inv - mean * mean
    rstd = lax.rsqrt(var + LN_EPS)
    vln = ((gv_ref[...] - mean) * rstd * lng_ref[...] + lnb_ref[...]).astype(BF16)
    ws = ws_ref[...].astype(BF16)
    bias = bs_ref[...]
    p = p_ref[...]
    ys = []
    for c in range(tm // C_CHUNK):
        rows = slice(c * C_CHUNK, (c + 1) * C_CHUNK)
        vm = jnp.dot(ws, vln[rows], preferred_element_type=F32) + bias
        ys.append((p[rows].astype(F32) * vm).astype(BF16))
    y_ref[s % 2, j] = jnp.concatenate(ys, axis=0)


def _gm_out(gv, p, s1, s2, ln_g, ln_b, w_s, b_s, w_out):
    m, cwid = gv.shape
    groups = w_s.shape[0]
    gwid = cwid // groups
    d = w_out.shape[1]
    tm = _pick(m, (512, 256))
    nrb = m // tm
    assert d % groups == 0 and (d // groups) % LANES == 0
    tn = d // groups

    def row(s):
        return jnp.minimum(s, nrb - 1)

    def out_idx(s, j):
        return (jnp.maximum(s - 1, 0), jnp.where(s == 0, 0, j))

    return pl.pallas_call(
        functools.partial(_gm_out_kernel, cwid=cwid, groups=groups),
        grid=(nrb + 1, groups),
        in_specs=[
            pl.BlockSpec((tm, gwid), lambda s, j: (row(s), j)),
            pl.BlockSpec((tm, gwid), lambda s, j: (row(s), j)),
            pl.BlockSpec((tm, LANES), lambda s, j: (row(s), 0)),
            pl.BlockSpec((tm, LANES), lambda s, j: (row(s), 0)),
            pl.BlockSpec((1, gwid), lambda s, j: (0, j)),
            pl.BlockSpec((1, gwid), lambda s, j: (0, j)),
            pl.BlockSpec((None, C_CHUNK, C_CHUNK), lambda s, j: (j, 0, 0)),
            pl.BlockSpec((None, C_CHUNK, 1), lambda s, j: (j, 0, 0)),
            pl.BlockSpec((groups, gwid, tn), lambda s, j: (0, 0, j)),
        ],
        out_specs=pl.BlockSpec((tm, tn), out_idx),
        out_shape=jax.ShapeDtypeStruct((m, d), F32),
        scratch_shapes=[pltpu.VMEM((2, groups, tm, gwid), BF16)],
        compiler_params=_params("arbitrary", "arbitrary"),
        name="gmlp_out",
    )(gv, p, s1, s2, ln_g.reshape(1, cwid), ln_b.reshape(1, cwid), w_s,
      b_s.reshape(groups, C_CHUNK, 1), w_out.reshape(groups, gwid, d))


def _final_kernel(x_ref, y_ref, gate_ref, g_ref, o_ref):
    x = x_ref[...] + gate_ref[...] * y_ref[...]
    ms = jnp.mean(x * x, axis=-1, keepdims=True)
    o_ref[...] = (x * lax.rsqrt(ms + NORM_EPS)) * g_ref[...]


def _final(x, y, mod, final_g):
    bsz, t, d = x.shape
    tm = 256
    mod3 = mod.reshape(mod.shape[0], 1, 3 * d)
    tok = pl.BlockSpec((None, tm, d), lambda b, j: (b, j, 0))
    return pl.pallas_call(
        _final_kernel,
        grid=(bsz, t // tm),
        in_specs=[tok, tok,
                  pl.BlockSpec((None, 1, d), lambda b, j: (b, 0, 2)),
                  pl.BlockSpec((1, d), lambda b, j: (0, 0))],
        out_specs=tok,
        out_shape=jax.ShapeDtypeStruct((bsz, t, d), F32),
        compiler_params=_params("parallel", "parallel"),
        name="final_norm",
    )(x, y.reshape(bsz, t, d), mod3, final_g.reshape(1, d))


def _ab_layout(d, heads, ab_in):
    aq = heads * A_HEAD_DIM
    bw = d // 2
    akv = (ab_in - 2 * aq - 4 * bw - 4 * LORA_RANK) // 2
    lay = {"aq": aq, "akv": akv, "bw": bw}
    lay.update(q=0, k=aq, v=aq + akv, width_a=aq + 2 * akv)
    off = 0
    for name, width in (("ga", aq), ("r", bw), ("kb", bw), ("vb", bw), ("gb", bw), ("lora", 2 * LORA_SLOT)):
        lay[name] = off
        off += width
    lay["width_b"] = off
    return lay


def _ab_weights(w_in, lay):
    aq, akv = lay["aq"], lay["akv"]
    qk = _rope_head_order(w_in[:, :aq + akv], (aq + akv) // A_HEAD_DIM)
    w_a = jnp.concatenate([qk, w_in[:, aq + akv:lay["width_a"]]], axis=1).astype(BF16)
    rest = w_in[:, lay["width_a"]:]
    w_b = jnp.pad(rest, ((0, 0), (0, lay["width_b"] - rest.shape[1]))).astype(BF16)
    return w_a, w_b


def kernel(x, c, ctx, c_ctx, mod_w, mod_b, norm_g, ab_w_in, ab_w_out, attn_sink, rwkv_conv, rwkv_w0, rwkv_w2,
           rwkv_a0, rwkv_a2, rwkv_k_k, rwkv_k_a, rwkv_r_k, rwkv_gn_w, rwkv_gn_b, gm_w_in, gm_ln_g, gm_ln_b,
           gm_w_s, gm_b_s, gm_w_out, final_g):
    bsz, n_lat, d = x.shape
    n_ctx = ctx.shape[1]
    heads = attn_sink.shape[1]

    cvec = jnp.concatenate([c, c_ctx[None, :], jnp.zeros((SUBLANES - bsz - 1, d), F32)], axis=0)
    mod = _modulation(cvec, mod_w, mod_b)

    lay = _ab_layout(d, heads, ab_w_in.shape[2])
    w_a, w_b = _ab_weights(ab_w_in[0], lay)
    t_all = n_ctx + n_lat
    h0 = _norm_mod(x, ctx, norm_g[0], mod[0], bsz).reshape(bsz * t_all, d)
    za = _matmul(h0, w_a).reshape(bsz, t_all, lay["width_a"])
    zb = _matmul(h0, w_b).reshape(bsz, t_all, lay["width_b"])
    mix_a = _attention(za, zb, attn_sink[0], _rope_tables(n_lat), lay, n_ctx, n_lat)
    r, v, kk, bonus, lw, kd, kb = _rwkv_prep(zb, lay, n_ctx, rwkv_conv[0], rwkv_w0[0], rwkv_w2[0], rwkv_a0[0],
                                             rwkv_a2[0], rwkv_k_k[0], rwkv_k_a[0], rwkv_r_k[0])
    yf, yb = _wkv(r, v, kk, lw, kd, kb, n_ctx, terms=1)
    mix_b = _readout(yf, yb, bonus, zb, lay, n_ctx, rwkv_gn_w[0], rwkv_gn_b[0])
    x1 = _out_proj(mix_a, mix_b, ab_w_out[0].astype(BF16), x, mod[0])

    cwid = gm_ln_g.shape[1]
    h1 = _norm_mod(x1, None, norm_g[1], mod[1], 0)
    p, gv, s1, s2 = _gm_in(h1.reshape(bsz * n_lat, d), gm_w_in[0].astype(BF16), cwid)
    y1 = _gm_out(gv, p, s1, s2, gm_ln_g[0], gm_ln_b[0], gm_w_s[0], gm_b_s[0], gm_w_out[0].astype(BF16))
    return _final(x1, y1, mod[1], final_g)
```

```python
import functools
import math

import jax
import jax.numpy as jnp
from jax import lax
from jax.experimental import pallas as pl
from jax.experimental.pallas import tpu as pltpu

F32 = jnp.float32
BF16 = jnp.bfloat16

A_HEAD_DIM = 128
A_WINDOW = 128
A_BLOCK = 128
GRID_W = 64
ROPE_THETA = 10000.0
B_HEAD_DIM = 64
LORA_RANK = 96
GN_EPS = 64e-5
C_CHUNK = 128
LN_EPS = 1e-5
NORM_EPS = 1e-6

LANES = 128
SUBLANES = 8
VMEM_LIMIT_BYTES = 56 * 1024 * 1024

SCAN_CHUNK = 32
SCAN_LEVELS = 5
SCAN_BLOCK = 64
SCAN_GROUP_WIDTH = 2048
LORA_SLOT = 256
LORA_LA_START = 128
MASK_VALUE = -1e30
DECAY_SCALE = math.exp(-0.5)


def _params(*semantics):
    return pltpu.CompilerParams(dimension_semantics=semantics, vmem_limit_bytes=VMEM_LIMIT_BYTES)


def _pick(n, candidates):
    for c in candidates:
        if n % c == 0:
            return c
    raise ValueError(f"no tile in {candidates} divides {n}")


def _split(x, terms):
    parts = []
    rest = x
    for i in range(terms):
        p = rest.astype(BF16)
        parts.append(p)
        if i + 1 < terms:
            rest = rest - p.astype(F32)
    return parts


_NN = (((1,), (0,)), ((), ()))
_NT = (((1,), (1,)), ((), ()))
_TN = (((0,), (0,)), ((), ()))


def _dotp(a, b, dims=_NN, ta=1, tb=1):
    pa = _split(a, ta)
    pb = _split(b, tb)
    out = None
    for i in range(ta):
        for j in range(tb):
            if i + j >= max(ta, tb):
                continue
            t = lax.dot_general(pa[i], pb[j], dims, preferred_element_type=F32)
            out = t if out is None else out + t
    return out


def _silu(x):
    return x * (0.5 * jnp.tanh(0.5 * x) + 0.5)


def _mod_kernel(c_ref, w_ref, b_ref, o_ref):
    s = _silu(c_ref[...])
    o_ref[...] = _dotp(s, w_ref[...], ta=2, tb=2) + b_ref[...]


def _modulation(cvec, mod_w, mod_b):
    depth, d, n = mod_w.shape
    rows = cvec.shape[0]
    tn = _pick(n, (512, 256, 128))
    return pl.pallas_call(
        _mod_kernel,
        grid=(depth, n // tn),
        in_specs=[
            pl.BlockSpec((rows, d), lambda l, j: (0, 0)),
            pl.BlockSpec((None, d, tn), lambda l, j: (l, 0, j)),
            pl.BlockSpec((None, 1, tn), lambda l, j: (l, 0, j)),
        ],
        out_specs=pl.BlockSpec((None, rows, tn), lambda l, j: (l, 0, j)),
        out_shape=jax.ShapeDtypeStruct((depth, rows, n), F32),
        compiler_params=_params("parallel", "parallel"),
        name="modulation",
    )(cvec, mod_w, mod_b.reshape(depth, 1, n))


def _norm_rows(x, g, shift, scale):
    ms = jnp.mean(x * x, axis=-1, keepdims=True)
    y = x * lax.rsqrt(ms + NORM_EPS)
    return (y * g) * (1.0 + scale) + shift


def _norm_mod_kernel(x_ref, c_ref, g_ref, shift_ref, scale_ref, o_ref, *, ctx_blocks):
    j = pl.program_id(1)

    @pl.when(j < ctx_blocks)
    def _():
        o_ref[...] = _norm_rows(c_ref[...], g_ref[...], shift_ref[...], scale_ref[...]).astype(BF16)

    @pl.when(j >= ctx_blocks)
    def _():
        o_ref[...] = _norm_rows(x_ref[...], g_ref[...], shift_ref[...], scale_ref[...]).astype(BF16)


def _norm_mod(x, ctx, g, mod, ctx_row):
    bsz, t, d = x.shape
    tm = 256
    if ctx is None:
        ctx = x
        cb = 0
    else:
        cb = ctx.shape[1] // tm
    nblk = cb + t // tm
    rows = mod.shape[0]
    mod3 = mod.reshape(rows, 1, 3 * d)

    def mrow(b, j):
        return jnp.where(j < cb, ctx_row, b)

    return pl.pallas_call(
        functools.partial(_norm_mod_kernel, ctx_blocks=cb),
        grid=(bsz, nblk),
        in_specs=[
            pl.BlockSpec((None, tm, d), lambda b, j: (b, jnp.maximum(j - cb, 0), 0)),
            pl.BlockSpec((None, tm, d), lambda b, j: (b, jnp.minimum(j, max(cb - 1, 0)), 0)),
            pl.BlockSpec((1, d), lambda b, j: (0, 0)),
            pl.BlockSpec((None, 1, d), lambda b, j: (mrow(b, j), 0, 0)),
            pl.BlockSpec((None, 1, d), lambda b, j: (mrow(b, j), 0, 1)),
        ],
        out_specs=pl.BlockSpec((None, tm, d), lambda b, j: (b, j, 0)),
        out_shape=jax.ShapeDtypeStruct((bsz, nblk * tm, d), BF16),
        compiler_params=_params("parallel", "parallel"),
        name="norm_mod",
    )(x, ctx, g.reshape(1, d), mod3, mod3)


def _mm_kernel(a_ref, b_ref, o_ref):
    o_ref[...] = jnp.dot(a_ref[...], b_ref[...], preferred_element_type=F32)


def _matmul(a, b):
    m, k = a.shape
    n = b.shape[1]
    tm = _pick(m, (1536, 1024, 768, 512, 256))
    tn = _pick(n, (512, 256, 128))
    return pl.pallas_call(
        _mm_kernel,
        grid=(m // tm, n // tn),
        in_specs=[
            pl.BlockSpec((tm, k), lambda i, j: (i, 0)),
            pl.BlockSpec((k, tn), lambda i, j: (0, j)),
        ],
        out_specs=pl.BlockSpec((tm, tn), lambda i, j: (i, j)),
        out_shape=jax.ShapeDtypeStruct((m, n), F32),
        compiler_params=_params("parallel", "parallel"),
        name="in_proj",
    )(a, b)


def _rope(x, cos, sin_signed):
    out = []
    for h in range(x.shape[1] // A_HEAD_DIM):
        xh = x[:, h * A_HEAD_DIM:(h + 1) * A_HEAD_DIM]
        out.append(xh * cos + pltpu.roll(xh, A_HEAD_DIM // 2, 1) * sin_signed)
    return out


def _rope_head_order(w, n_heads):
    w5 = w.reshape(w.shape[0], n_heads, 2, 2, A_HEAD_DIM // 4)
    return jnp.swapaxes(w5, 2, 3).reshape(w.shape)


def _attn_kernel(sink_ref, q_ref, g_ref, kp_ref, kc_ref, kn_ref, vp_ref, vc_ref, vn_ref,
                 kx_ref, vx_ref, cq_ref, sq_ref, cp_ref, sp_ref, cn_ref, sn_ref, o_ref,
                 *, n_blocks, kv_heads, group):
    n = pl.program_id(1)
    blk = A_BLOCK
    lctx = kx_ref.shape[0]
    scale = A_HEAD_DIM ** -0.5

    q_heads = _rope(q_ref[...], cq_ref[...], sq_ref[...])
    kp = _rope(kp_ref[...], cp_ref[...], sp_ref[...])
    kc = _rope(kc_ref[...], cq_ref[...], sq_ref[...])
    kn = _rope(kn_ref[...], cn_ref[...], sn_ref[...])

    qi = lax.broadcasted_iota(jnp.int32, (blk, lctx + 3 * blk), 0)
    kj = lax.broadcasted_iota(jnp.int32, (blk, lctx + 3 * blk), 1) - lctx
    no_prev = jnp.where(n > 0, 0, blk)
    no_next = jnp.where(n < n_blocks - 1, 0, blk)
    prev_ok = (kj >= qi + no_prev) & (kj < blk)
    next_ok = (kj - 2 * blk <= qi - no_next) & (kj >= 2 * blk)
    valid = (kj < 0) | prev_ok | ((kj >= blk) & (kj < 2 * blk)) | next_ok
    mask = jnp.where(valid, 0.0, MASK_VALUE).astype(F32)
    mask = jnp.concatenate([mask] * group, axis=0)

    gate = g_ref[...]
    scores, sinks, probs, dens = [], [], [], []
    for hk in range(kv_heads):
        sl = slice(hk * A_HEAD_DIM, (hk + 1) * A_HEAD_DIM)
        kcat = jnp.concatenate([kx_ref[:, sl], kp[hk], kc[hk], kn[hk]], axis=0).astype(BF16)
        qs = jnp.concatenate([q_heads[hk * group + g] * scale for g in range(group)], axis=0).astype(BF16)
        scores.append(lax.dot_general(qs, kcat, _NT, preferred_element_type=F32) + mask)
        sinks.append(jnp.concatenate(
            [jnp.full((blk, 1), sink_ref[hk * group + g], F32) for g in range(group)], axis=0))
    for hk in range(kv_heads):
        m = jnp.maximum(jnp.max(scores[hk], axis=-1, keepdims=True), sinks[hk])
        p = jnp.exp(scores[hk] - m)
        dens.append(jnp.exp(sinks[hk] - m) + jnp.sum(p, axis=-1, keepdims=True))
        probs.append(p.astype(BF16))
    for hk in range(kv_heads):
        sl = slice(hk * A_HEAD_DIM, (hk + 1) * A_HEAD_DIM)
        vcat = jnp.concatenate([vx_ref[:, sl], vp_ref[:, sl], vc_ref[:, sl], vn_ref[:, sl]], axis=0).astype(BF16)
        o = jnp.dot(probs[hk], vcat, preferred_element_type=F32) / dens[hk]
        for g in range(group):
            h = hk * group + g
            hs = slice(h * A_HEAD_DIM, (h + 1) * A_HEAD_DIM)
            o_ref[:, hs] = (o[g * blk:(g + 1) * blk] * _silu(gate[:, hs])).astype(BF16)


def _attention(za, zb, sink, tables, lay, n_ctx, n_lat):
    bsz = za.shape[0]
    aq, akv = lay["aq"], lay["akv"]
    heads = aq // A_HEAD_DIM
    kv_heads = akv // A_HEAD_DIM
    blk = A_BLOCK
    nb = n_lat // blk
    off = n_ctx // blk
    kcol = lay["k"] // akv
    vcol = lay["v"] // akv
    cos, sin = tables

    def prev(n):
        return jnp.maximum(n - 1, 0)

    def nxt(n):
        return jnp.minimum(n + 1, nb - 1)

    def zspec(width, rowf, col):
        return pl.BlockSpec((None, blk, width), lambda b, n: (b, rowf(n) + off, col))

    def tspec(rowf):
        return pl.BlockSpec((blk, A_HEAD_DIM), lambda b, n: (rowf(n), 0))

    ident = lambda n: n
    return pl.pallas_call(
        functools.partial(_attn_kernel, n_blocks=nb, kv_heads=kv_heads, group=heads // kv_heads),
        grid=(bsz, nb),
        in_specs=[
            pl.BlockSpec(memory_space=pltpu.SMEM),
            zspec(aq, ident, lay["q"] // aq),
            zspec(aq, ident, lay["ga"] // aq),
            zspec(akv, prev, kcol), zspec(akv, ident, kcol), zspec(akv, nxt, kcol),
            zspec(akv, prev, vcol), zspec(akv, ident, vcol), zspec(akv, nxt, vcol),
            pl.BlockSpec((None, n_ctx, akv), lambda b, n: (b, 0, kcol)),
            pl.BlockSpec((None, n_ctx, akv), lambda b, n: (b, 0, vcol)),
            tspec(ident), tspec(ident), tspec(prev), tspec(prev), tspec(nxt), tspec(nxt),
        ],
        out_specs=pl.BlockSpec((None, blk, aq), lambda b, n: (b, n, 0)),
        out_shape=jax.ShapeDtypeStruct((bsz, n_lat, aq), BF16),
        compiler_params=_params("parallel", "parallel"),
        name="band_attention",
    )(sink, za, zb, za, za, za, za, za, za, za, za, cos, sin, cos, sin, cos, sin)


def _rope_tables(n_tok):
    rows = n_tok // GRID_W
    row = jnp.repeat(jnp.arange(rows, dtype=F32), GRID_W)
    col = jnp.tile(jnp.arange(GRID_W, dtype=F32), rows)
    n_freq = A_HEAD_DIM // 4
    inv_freq = ROPE_THETA ** (-jnp.arange(n_freq, dtype=F32) / n_freq)
    ar = row[:, None] * inv_freq
    ac = col[:, None] * inv_freq
    cos = jnp.concatenate([jnp.cos(ar), jnp.cos(ac), jnp.cos(ar), jnp.cos(ac)], axis=-1)
    sin = jnp.concatenate([-jnp.sin(ar), -jnp.sin(ac), jnp.sin(ar), jnp.sin(ac)], axis=-1)
    return cos, sin


def _head_sum(x, ones_bd):
    out = []
    for i in range(x.shape[1] // LANES):
        out.append(_dotp(x[:, i * LANES:(i + 1) * LANES], ones_bd, ta=2, tb=1))
    return jnp.concatenate(out, axis=1) if len(out) > 1 else out[0]


def _head_ones():
    r = lax.broadcasted_iota(jnp.int32, (LANES, LANES), 0) // B_HEAD_DIM
    c = lax.broadcasted_iota(jnp.int32, (LANES, LANES), 1) // B_HEAD_DIM
    return (r == c).astype(F32)


def _prep_kernel(zr_ref, zk_ref, zv_ref, zrp_ref, zkp_ref, zvp_ref, zrn_ref, zkn_ref, zvn_ref,
                 zwa_ref, cwr_ref, cwk_ref, cwv_ref, w2_ref, a2_ref,
                 w0_ref, a0_ref, kk_ref, ka_ref, rk_ref,
                 r_o, v_o, kk_o, bonus_o, lw_o, kd_o, kb_o, *, ctx_blocks, n_blocks):
    j = pl.program_id(1)
    tm = zr_ref.shape[0]
    has_prev = jnp.logical_and(j != 0, j != ctx_blocks)
    has_next = jnp.logical_and(j != ctx_blocks - 1, j != n_blocks - 1)
    rows = lax.broadcasted_iota(jnp.int32, zr_ref.shape, 0)

    def conv(x_ref, p_ref, n_ref, w_ref):
        x = x_ref[...]
        w = w_ref[...]
        before = jnp.where(has_prev, p_ref[SUBLANES - 1:SUBLANES, :], 0.0)
        after = jnp.where(has_next, n_ref[0:1, :], 0.0)
        xm = jnp.where(rows == 0, before, pltpu.roll(x, 1, 0))
        xp = jnp.where(rows == tm - 1, after, pltpu.roll(x, tm - 1, 0))
        return xm * w[0:1] + x * w[1:2] + xp * w[2:3]

    r = conv(zr_ref, zrp_ref, zrn_ref, cwr_ref)
    k = conv(zk_ref, zkp_ref, zkn_ref, cwk_ref)
    v = conv(zv_ref, zvp_ref, zvn_ref, cwv_ref)
    ones_bd = _head_ones()

    kx = k * kk_ref[...]
    kkn = kx / jnp.maximum(jnp.sqrt(_head_sum(kx * kx, ones_bd)), 1e-12)
    r_o[...] = r.astype(BF16)
    v_o[...] = v.astype(BF16)
    kk_o[...] = kkn.astype(BF16)
    bonus_o[...] = _head_sum(r * k * rk_ref[...], ones_bd) * v

    zwa = zwa_ref[...]
    lw = _split(jnp.tanh(zwa[:, :LORA_SLOT]), 2)
    la = _split(zwa[:, LORA_LA_START:LORA_LA_START + LORA_SLOT], 2)
    ka = ka_ref[...]

    def lora(x, w_ref, d):
        out = jnp.dot(x[0], w_ref[d, 0], preferred_element_type=F32)
        out = out + jnp.dot(x[0], w_ref[d, 1], preferred_element_type=F32)
        return out + jnp.dot(x[1], w_ref[d, 0], preferred_element_type=F32)

    for d in range(2):
        xw = w0_ref[d:d + 1, :] + lora(lw, w2_ref, d)
        lw_o[d] = -DECAY_SCALE * jax.nn.sigmoid(xw)
        a = jax.nn.sigmoid(a0_ref[d:d + 1, :] + lora(la, a2_ref, d))
        kd_o[d] = (k * (1.0 + (a - 1.0) * ka)).astype(BF16)
        kb_o[d] = (kkn * a).astype(BF16)


def _rwkv_prep(z, lay, n_ctx, conv_w, w0, w2, a0, a2, k_k, k_a, r_k):
    bsz, t_all, _ = z.shape
    bw = lay["bw"]
    tm = 256
    cw = _pick(bw, (512, 256, 128))
    nblk = t_all // tm
    cb = n_ctx // tm
    sub = tm // SUBLANES
    last8 = t_all // SUBLANES - 1

    def main(off):
        return pl.BlockSpec((None, tm, cw), lambda b, j, c: (b, j, off // cw + c))

    def before(off):
        return pl.BlockSpec((None, SUBLANES, cw), lambda b, j, c: (b, jnp.maximum(j * sub - 1, 0), off // cw + c))

    def after(off):
        return pl.BlockSpec((None, SUBLANES, cw), lambda b, j, c: (b, jnp.minimum((j + 1) * sub, last8), off // cw + c))

    def chan(rows, off=0):
        return pl.BlockSpec((rows, cw), lambda b, j, c: (0, off // cw + c))

    def pad_lora(w, first_row):
        full = jnp.zeros((2, LORA_SLOT, bw), F32)
        for d in range(2):
            full = full.at[d, first_row + d * LORA_RANK:first_row + (d + 1) * LORA_RANK].set(w[d])
        hi = full.astype(BF16)
        lo = (full - hi.astype(F32)).astype(BF16)
        return jnp.stack([hi, lo], axis=1)

    lora_spec = pl.BlockSpec((2, 2, LORA_SLOT, cw), lambda b, j, c: (0, 0, 0, c))
    tok = pl.BlockSpec((None, tm, cw), lambda b, j, c: (b, j, c))
    tok2 = pl.BlockSpec((2, None, tm, cw), lambda b, j, c: (0, b, j, c))
    one = jax.ShapeDtypeStruct((bsz, t_all, bw), BF16)
    one32 = jax.ShapeDtypeStruct((bsz, t_all, bw), F32)
    two = jax.ShapeDtypeStruct((2, bsz, t_all, bw), BF16)
    two32 = jax.ShapeDtypeStruct((2, bsz, t_all, bw), F32)
    return pl.pallas_call(
        functools.partial(_prep_kernel, ctx_blocks=cb, n_blocks=nblk),
        grid=(bsz, nblk, bw // cw),
        in_specs=[
            main(lay["r"]), main(lay["kb"]), main(lay["vb"]),
            before(lay["r"]), before(lay["kb"]), before(lay["vb"]),
            after(lay["r"]), after(lay["kb"]), after(lay["vb"]),
            pl.BlockSpec((None, tm, 2 * LORA_SLOT), lambda b, j, c: (b, j, lay["lora"] // (2 * LORA_SLOT))),
            chan(3, 0), chan(3, bw), chan(3, 2 * bw),
            lora_spec, lora_spec,
            chan(2), chan(2), chan(1), chan(1), chan(1),
        ],
        out_specs=[tok, tok, tok, tok, tok2, tok2, tok2],
        out_shape=[one, one, one, one32, two32, two, two],
        compiler_params=_params("parallel", "parallel", "parallel"),
        name="rwkv_prep",
    )(z, z, z, z, z, z, z, z, z, z, conv_w, conv_w, conv_w,
      pad_lora(w2, 0), pad_lora(a2, 2 * LORA_RANK - LORA_LA_START),
      w0, a0, k_k.reshape(1, bw), k_a.reshape(1, bw), r_k.reshape(1, bw))


def _stack_heads(x, lane_is_first):
    return jnp.concatenate([jnp.where(lane_is_first, x, 0.0), jnp.where(lane_is_first, 0.0, x)], axis=0)


def _wkv_operands(r, v, kk, lw, kd, kb, reverse):
    nb, ch = SCAN_BLOCK, SCAN_CHUNK
    row = lax.broadcasted_iota(jnp.int32, (nb, nb), 0)
    col = lax.broadcasted_iota(jnp.int32, (nb, nb), 1)
    incl = ((row <= col) if reverse else (row >= col)) & ((row // ch) == (col // ch))
    cum = _dotp(incl.astype(F32), lw, ta=1, tb=3)
    lasts = [cum[i * ch:i * ch + 1, :] if reverse else cum[(i + 1) * ch - 1:(i + 1) * ch, :]
             for i in range(nb // ch)]
    last = jnp.concatenate([jnp.broadcast_to(l, (ch, l.shape[1])) for l in lasts], axis=0)
    g_in = jnp.exp(cum)
    g_inv = jnp.exp(-cum)
    g_ex = jnp.exp(cum - lw)
    g_rem = jnp.exp(last - cum)
    return dict(a=-kk * g_ex, r=r * g_in, b=kb * g_inv, k=kd * g_inv, v=v, kg=kd * g_rem, bg=kb * g_rem,
                g_end=[jnp.exp(l) for l in lasts])


def _wkv_step(streams, terms):
    nb, ch = SCAN_BLOCK, SCAN_CHUNK
    two = 2 * ch
    assert 2 * two == LANES
    dot = functools.partial(_dotp, ta=terms, tb=terms)
    ti = lax.broadcasted_iota(jnp.int32, (two, LANES), 0) % ch
    tj = lax.broadcasted_iota(jnp.int32, (two, LANES), 1) % ch
    lane = lax.broadcasted_iota(jnp.int32, (two, LANES), 1)
    first = lax.broadcasted_iota(jnp.int32, (ch, LANES), 1) < B_HEAD_DIM
    st = functools.partial(_stack_heads, lane_is_first=first)
    zeros = jnp.zeros((two, LANES), F32)

    chains = []
    for ops, s_ref, y_ref, reverse in streams:
        m_incl = (ti <= tj) if reverse else (ti >= tj)
        m_strict = (ti < tj) if reverse else (ti > tj)
        for p in range(ops["a"].shape[1] // LANES):
            chains.append(dict(ops=ops, sl=slice(p * LANES, (p + 1) * LANES), p=p, s_ref=s_ref, y_ref=y_ref,
                               m_incl=m_incl, m_strict=m_strict, reverse=reverse, state=s_ref[p]))

    n_chunks = nb // ch
    for step in range(n_chunks):
        for c in chains:
            i = n_chunks - 1 - step if c["reverse"] else step
            c["rows"] = slice(i * ch, (i + 1) * ch)
            c["g_end"] = c["ops"]["g_end"][i][:, c["sl"]]
            for n in ("a", "r", "b", "k", "v", "kg", "bg"):
                c[n] = st(c["ops"][n][c["rows"], c["sl"]])
        for c in chains:
            lhs = jnp.concatenate([c["a"], c["r"]], axis=0)
            res = dot(lhs, jnp.concatenate([c["b"], c["k"], c["state"]], axis=0), _NT)
            c["d_a"] = jnp.where(c["m_strict"], res[:two, :LANES], 0.0)
            c["d_r"] = jnp.where(c["m_incl"], res[two:, :LANES], 0.0)
            c["from_state"] = res[:, LANES:]
        for c in chains:
            c["x"] = c["from_state"][:two] + dot(c["d_a"], jnp.concatenate([zeros, c["v"]], axis=0))
            c["d"] = jnp.where(lane < two, c["d_a"], 0.0)
        for lvl in range(SCAN_LEVELS):
            for c in chains:
                if lvl + 1 < SCAN_LEVELS:
                    rhs = jnp.concatenate([jnp.concatenate([c["d"], c["x"]], axis=1),
                                           jnp.zeros((two, 2 * LANES), F32)], axis=0)
                    res = dot(c["d"], rhs)
                    c["d"] = res[:, :LANES]
                    c["x"] = c["x"] + res[:, LANES:]
                else:
                    c["x"] = c["x"] + dot(c["d"], jnp.concatenate([c["x"], zeros], axis=0))
        for c in chains:
            uv = jnp.concatenate([c["x"], c["v"]], axis=0)
            y_s = c["from_state"][two:] + dot(c["d_r"], uv)
            c["y_ref"][c["rows"], c["sl"]] = (y_s[:ch] + y_s[ch:]).astype(BF16)
            c["state"] = c["state"] * c["g_end"] + dot(uv, jnp.concatenate([c["bg"], c["kg"]], axis=0), _TN)
    for c in chains:
        c["s_ref"][c["p"]] = c["state"]


def _wkv_kernel(rf_ref, vf_ref, kkf_ref, lwf_ref, kdf_ref, kbf_ref,
                rb_ref, vb_ref, kkb_ref, lwb_ref, kdb_ref, kbb_ref,
                yf_ref, yb_ref, sf_ref, sb_ref, *, terms):
    @pl.when(pl.program_id(2) == 0)
    def _():
        sf_ref[...] = jnp.zeros_like(sf_ref)
        sb_ref[...] = jnp.zeros_like(sb_ref)

    def f32(ref):
        return ref[...].astype(F32)

    fwd = _wkv_operands(f32(rf_ref), f32(vf_ref), f32(kkf_ref), lwf_ref[...], f32(kdf_ref), f32(kbf_ref), False)
    bwd = _wkv_operands(f32(rb_ref), f32(vb_ref), f32(kkb_ref), lwb_ref[...], f32(kdb_ref), f32(kbb_ref), True)
    _wkv_step([(fwd, sf_ref, yf_ref, False), (bwd, sb_ref, yb_ref, True)], terms)


def _wkv(r, v, kk, lw, kd, kb, n_ctx, terms):
    bsz, t_all, bw = r.shape
    ch = SCAN_BLOCK
    gw = _pick(bw, (SCAN_GROUP_WIDTH, 256, 128))
    nc = t_all // ch
    ncc = n_ctx // ch

    def fwd(c):
        return c

    def bwd(c):
        return jnp.where(c < ncc, ncc - 1 - c, nc + ncc - 1 - c)

    def tok(order):
        return pl.BlockSpec((None, ch, gw), lambda b, g, c: (b, order(c), g))

    def tok2(order, d):
        return pl.BlockSpec((None, None, ch, gw), lambda b, g, c: (d, b, order(c), g))

    out = jax.ShapeDtypeStruct((bsz, t_all, bw), BF16)
    state = pltpu.VMEM((gw // LANES, LANES, LANES), F32)
    return pl.pallas_call(
        functools.partial(_wkv_kernel, terms=terms),
        grid=(bsz, bw // gw, nc),
        in_specs=[tok(fwd), tok(fwd), tok(fwd), tok2(fwd, 0), tok2(fwd, 0), tok2(fwd, 0),
                  tok(bwd), tok(bwd), tok(bwd), tok2(bwd, 1), tok2(bwd, 1), tok2(bwd, 1)],
        out_specs=[tok(fwd), tok(bwd)],
        out_shape=[out, out],
        scratch_shapes=[state, state],
        compiler_params=_params("parallel", "parallel", "arbitrary"),
        name="wkv_scan",
    )(r, v, kk, lw, kd, kb, r, v, kk, lw, kd, kb)


def _readout_kernel(yf_ref, yb_ref, bonus_ref, gate_ref, gw_ref, gb_ref, o_ref):
    ones_bd = _head_ones()
    y = yf_ref[...].astype(F32) + yb_ref[...].astype(F32)
    inv = 1.0 / B_HEAD_DIM
    mean = _head_sum(y, ones_bd) * inv
    yc = y - mean
    var = _head_sum(yc * yc, ones_bd) * inv
    out = yc * lax.rsqrt(var + GN_EPS) * gw_ref[...] + gb_ref[...]
    out = out + bonus_ref[...]
    o_ref[...] = (out * _silu(gate_ref[...])).astype(BF16)


def _readout(yf, yb, bonus, z, lay, n_ctx, gn_w, gn_b):
    bsz, t_all, bw = yf.shape
    n_lat = t_all - n_ctx
    tm = 256
    cw = _pick(bw, (2048, 512, 256, 128))
    off = n_ctx // tm
    tok = pl.BlockSpec((None, tm, cw), lambda b, j, c: (b, j + off, c))
    chan = pl.BlockSpec((1, cw), lambda b, j, c: (0, c))
    return pl.pallas_call(
        _readout_kernel,
        grid=(bsz, n_lat // tm, bw // cw),
        in_specs=[tok, tok, tok,
                  pl.BlockSpec((None, tm, cw), lambda b, j, c: (b, j + off, lay["gb"] // cw + c)),
                  chan, chan],
        out_specs=pl.BlockSpec((None, tm, cw), lambda b, j, c: (b, j, c)),
        out_shape=jax.ShapeDtypeStruct((bsz, n_lat, bw), BF16),
        compiler_params=_params("parallel", "parallel", "parallel"),
        name="rwkv_readout",
    )(yf, yb, bonus, z, gn_w.reshape(1, bw), gn_b.reshape(1, bw))


def _out_proj_kernel(a_ref, b_ref, wa_ref, wb_ref, x_ref, gate_ref, o_ref):
    y = jnp.dot(a_ref[...], wa_ref[...], preferred_element_type=F32)
    y = y + jnp.dot(b_ref[...], wb_ref[...], preferred_element_type=F32)
    o_ref[...] = x_ref[...] + gate_ref[...] * y


def _out_proj(mix_a, mix_b, w_out, x, mod):
    bsz, t, d = x.shape
    ka, kb = mix_a.shape[2], mix_b.shape[2]
    assert ka % kb == 0 and w_out.shape[0] == ka + kb
    tm = _pick(t, (1024, 512, 256))
    tn = _pick(d, (512, 256, 128))
    mod3 = mod.reshape(mod.shape[0], 1, 3 * d)
    return pl.pallas_call(
        _out_proj_kernel,
        grid=(bsz, t // tm, d // tn),
        in_specs=[
            pl.BlockSpec((None, tm, ka), lambda b, i, j: (b, i, 0)),
            pl.BlockSpec((None, tm, kb), lambda b, i, j: (b, i, 0)),
            pl.BlockSpec((ka, tn), lambda b, i, j: (0, j)),
            pl.BlockSpec((kb, tn), lambda b, i, j: (ka // kb, j)),
            pl.BlockSpec((None, tm, tn), lambda b, i, j: (b, i, j)),
            pl.BlockSpec((None, 1, tn), lambda b, i, j: (b, 0, 2 * (d // tn) + j)),
        ],
        out_specs=pl.BlockSpec((None, tm, tn), lambda b, i, j: (b, i, j)),
        out_shape=jax.ShapeDtypeStruct((bsz, t, d), F32),
        compiler_params=_params("parallel", "parallel", "parallel"),
        name="out_proj",
    )(mix_a, mix_b, w_out, w_out, x, mod3)


def _fold_lanes(x):
    out = x[:, :LANES]
    for i in range(1, x.shape[1] // LANES):
        out = out + x[:, i * LANES:(i + 1) * LANES]
    return out


def _gm_in_kernel(h_ref, wu_ref, wv_ref, wg_ref, p_ref, gv_ref, s1_ref, s2_ref):
    @pl.when(pl.program_id(1) == 0)
    def _():
        s1_ref[...] = jnp.zeros_like(s1_ref)
        s2_ref[...] = jnp.zeros_like(s2_ref)

    h = h_ref[...]
    gv = jax.nn.gelu(jnp.dot(h, wv_ref[...], preferred_element_type=F32))
    gv_ref[...] = gv.astype(BF16)
    s1_ref[...] += _fold_lanes(gv)
    s2_ref[...] += _fold_lanes(gv * gv)
    gu = jax.nn.gelu(jnp.dot(h, wu_ref[...], preferred_element_type=F32))
    g = jnp.dot(h, wg_ref[...], preferred_element_type=F32)
    p_ref[...] = (gu * _silu(g)).astype(BF16)


def _gm_in(h, w_in, cwid):
    m, d = h.shape
    tm = _pick(m, (1024, 512, 256))
    tn = _pick(cwid, (512, 256, 128))
    nt = cwid // tn
    stat = jax.ShapeDtypeStruct((m, LANES), F32)
    return pl.pallas_call(
        _gm_in_kernel,
        grid=(m // tm, nt),
        in_specs=[
            pl.BlockSpec((tm, d), lambda i, j: (i, 0), pipeline_mode=pl.Buffered(1)),
            pl.BlockSpec((d, tn), lambda i, j: (0, j)),
            pl.BlockSpec((d, tn), lambda i, j: (0, nt + j)),
            pl.BlockSpec((d, tn), lambda i, j: (0, 2 * nt + j)),
        ],
        out_specs=[
            pl.BlockSpec((tm, tn), lambda i, j: (i, j)),
            pl.BlockSpec((tm, tn), lambda i, j: (i, j)),
            pl.BlockSpec((tm, LANES), lambda i, j: (i, 0)),
            pl.BlockSpec((tm, LANES), lambda i, j: (i, 0)),
        ],
        out_shape=[jax.ShapeDtypeStruct((m, cwid), BF16), jax.ShapeDtypeStruct((m, cwid), BF16), stat, stat],
        compiler_params=_params("parallel", "arbitrary"),
        name="gmlp_in",
    )(h, w_in, w_in, w_in)


def _gm_out_kernel(gv_ref, p_ref, s1_ref, s2_ref, lng_ref, lnb_ref, ws_ref, bs_ref, wo_ref, o_ref, y_ref,
                   *, cwid, groups):
    s = pl.program_id(0)
    j = pl.program_id(1)
    tm = gv_ref.shape[0]

    @pl.when(jnp.logical_and(s == 0, j == 0))
    def _():
        y_ref[...] = jnp.zeros_like(y_ref)

    prev = (s + 1) % 2
    acc = jnp.dot(y_ref[prev, 0], wo_ref[0], preferred_element_type=F32)
    for g in range(1, groups):
        acc = acc + jnp.dot(y_ref[prev, g], wo_ref[g], preferred_element_type=F32)
    o_ref[...] = acc

    inv = 1.0 / cwid
    mean = jnp.sum(s1_ref[...], axis=-1, keepdims=True) * inv
    var = jnp.sum(s2_ref[...], axis=-1, keepdims=True) * inv - mean * mean
    rstd = lax.rsqrt(var + LN_EPS)
    vln = ((gv_ref[...].astype(F32) - mean) * rstd * lng_ref[...] + lnb_ref[...]).astype(BF16)
    ws = ws_ref[...].astype(BF16)
    bias = bs_ref[...]
    p = p_ref[...]
    ys = []
    for c in range(tm // C_CHUNK):
        rows = slice(c * C_CHUNK, (c + 1) * C_CHUNK)
        vm = jnp.dot(ws, vln[rows], preferred_element_type=F32) + bias
        ys.append((p[rows].astype(F32) * vm).astype(BF16))
    y_ref[s % 2, j] = jnp.concatenate(ys, axis=0)


def _gm_out(gv, p, s1, s2, ln_g, ln_b, w_s, b_s, w_out):
    m, cwid = gv.shape
    groups = w_s.shape[0]
    gwid = cwid // groups
    d = w_out.shape[1]
    tm = _pick(m, (512, 256))
    nrb = m // tm
    assert d % groups == 0 and (d // groups) % LANES == 0
    tn = d // groups

    def row(s):
        return jnp.minimum(s, nrb - 1)

    def out_idx(s, j):
        return (jnp.maximum(s - 1, 0), jnp.where(s == 0, 0, j))

    return pl.pallas_call(
        functools.partial(_gm_out_kernel, cwid=cwid, groups=groups),
        grid=(nrb + 1, groups),
        in_specs=[
            pl.BlockSpec((tm, gwid), lambda s, j: (row(s), j)),
            pl.BlockSpec((tm, gwid), lambda s, j: (row(s), j)),
            pl.BlockSpec((tm, LANES), lambda s, j: (row(s), 0)),
            pl.BlockSpec((tm, LANES), lambda s, j: (row(s), 0)),
            pl.BlockSpec((1, gwid), lambda s, j: (0, j)),
            pl.BlockSpec((1, gwid), lambda s, j: (0, j)),
            pl.BlockSpec((None, C_CHUNK, C_CHUNK), lambda s, j: (j, 0, 0)),
            pl.BlockSpec((None, C_CHUNK, 1), lambda s, j: (j, 0, 0)),
            pl.BlockSpec((groups, gwid, tn), lambda s, j: (0, 0, j)),
        ],
        out_specs=pl.BlockSpec((tm, tn), out_idx),
        out_shape=jax.ShapeDtypeStruct((m, d), F32),
        scratch_shapes=[pltpu.VMEM((2, groups, tm, gwid), BF16)],
        compiler_params=_params("arbitrary", "arbitrary"),
        name="gmlp_out",
    )(gv, p, s1, s2, ln_g.reshape(1, cwid), ln_b.reshape(1, cwid), w_s,
      b_s.reshape(groups, C_CHUNK, 1), w_out.reshape(groups, gwid, d))


def _final_kernel(x_ref, y_ref, gate_ref, g_ref, o_ref):
    x = x_ref[...] + gate_ref[...] * y_ref[...]
    ms = jnp.mean(x * x, axis=-1, keepdims=True)
    o_ref[...] = (x * lax.rsqrt(ms + NORM_EPS)) * g_ref[...]


def _final(x, y, mod, final_g):
    bsz, t, d = x.shape
    tm = 256
    mod3 = mod.reshape(mod.shape[0], 1, 3 * d)
    tok = pl.BlockSpec((None, tm, d), lambda b, j: (b, j, 0))
    return pl.pallas_call(
        _final_kernel,
        grid=(bsz, t // tm),
        in_specs=[tok, tok,
                  pl.BlockSpec((None, 1, d), lambda b, j: (b, 0, 2)),
                  pl.BlockSpec((1, d), lambda b, j: (0, 0))],
        out_specs=tok,
        out_shape=jax.ShapeDtypeStruct((bsz, t, d), F32),
        compiler_params=_params("parallel", "parallel"),
        name="final_norm",
    )(x, y.reshape(bsz, t, d), mod3, final_g.reshape(1, d))


def _ab_layout(d, heads, ab_in):
    aq = heads * A_HEAD_DIM
    bw = d // 2
    akv = (ab_in - 2 * aq - 4 * bw - 4 * LORA_RANK) // 2
    lay = {"aq": aq, "akv": akv, "bw": bw}
    lay.update(q=0, k=aq, v=aq + akv, width_a=aq + 2 * akv)
    off = 0
    for name, width in (("ga", aq), ("r", bw), ("kb", bw), ("vb", bw), ("gb", bw), ("lora", 2 * LORA_SLOT)):
        lay[name] = off
        off += width
    lay["width_b"] = off
    return lay


def _ab_weights(w_in, lay):
    aq, akv = lay["aq"], lay["akv"]
    qk = _rope_head_order(w_in[:, :aq + akv], (aq + akv) // A_HEAD_DIM)
    w_a = jnp.concatenate([qk, w_in[:, aq + akv:lay["width_a"]]], axis=1).astype(BF16)
    rest = w_in[:, lay["width_a"]:]
    w_b = jnp.pad(rest, ((0, 0), (0, lay["width_b"] - rest.shape[1]))).astype(BF16)
    return w_a, w_b


def kernel(x, c, ctx, c_ctx, mod_w, mod_b, norm_g, ab_w_in, ab_w_out, attn_sink, rwkv_conv, rwkv_w0, rwkv_w2,
           rwkv_a0, rwkv_a2, rwkv_k_k, rwkv_k_a, rwkv_r_k, rwkv_gn_w, rwkv_gn_b, gm_w_in, gm_ln_g, gm_ln_b,
           gm_w_s, gm_b_s, gm_w_out, final_g):
    bsz, n_lat, d = x.shape
    n_ctx = ctx.shape[1]
    heads = attn_sink.shape[1]

    cvec = jnp.concatenate([c, c_ctx[None, :], jnp.zeros((SUBLANES - bsz - 1, d), F32)], axis=0)
    mod = _modulation(cvec, mod_w, mod_b)

    lay = _ab_layout(d, heads, ab_w_in.shape[2])
    w_a, w_b = _ab_weights(ab_w_in[0], lay)
    t_all = n_ctx + n_lat
    h0 = _norm_mod(x, ctx, norm_g[0], mod[0], bsz).reshape(bsz * t_all, d)
    za = _matmul(h0, w_a).reshape(bsz, t_all, lay["width_a"])
    zb = _matmul(h0, w_b).reshape(bsz, t_all, lay["width_b"])
    mix_a = _attention(za, zb, attn_sink[0], _rope_tables(n_lat), lay, n_ctx, n_lat)
    r, v, kk, bonus, lw, kd, kb = _rwkv_prep(zb, lay, n_ctx, rwkv_conv[0], rwkv_w0[0], rwkv_w2[0], rwkv_a0[0],
                                             rwkv_a2[0], rwkv_k_k[0], rwkv_k_a[0], rwkv_r_k[0])
    yf, yb = _wkv(r, v, kk, lw, kd, kb, n_ctx, terms=1)
    mix_b = _readout(yf, yb, bonus, zb, lay, n_ctx, rwkv_gn_w[0], rwkv_gn_b[0])
    x1 = _out_proj(mix_a, mix_b, ab_w_out[0].astype(BF16), x, mod[0])

    cwid = gm_ln_g.shape[1]
    h1 = _norm_mod(x1, None, norm_g[1], mod[1], 0)
    p, gv, s1, s2 = _gm_in(h1.reshape(bsz * n_lat, d), gm_w_in[0].astype(BF16), cwid)
    y1 = _gm_out(gv, p, s1, s2, gm_ln_g[0], gm_ln_b[0], gm_w_s[0], gm_b_s[0], gm_w_out[0].astype(BF16))
    return _final(x1, y1, mod[1], final_g)
```

```python
import functools
import math

import jax
import jax.numpy as jnp
from jax import lax
from jax.experimental import pallas as pl
from jax.experimental.pallas import tpu as pltpu

F32 = jnp.float32
BF16 = jnp.bfloat16

A_HEAD_DIM = 128
A_WINDOW = 128
A_BLOCK = 128
GRID_W = 64
ROPE_THETA = 10000.0
B_HEAD_DIM = 64
LORA_RANK = 96
GN_EPS = 64e-5
C_CHUNK = 128
LN_EPS = 1e-5
NORM_EPS = 1e-6

LANES = 128
SUBLANES = 8
VMEM_LIMIT_BYTES = 56 * 1024 * 1024

SCAN_CHUNK = 32
SCAN_LEVELS = 5
SCAN_BLOCK = 64
SCAN_GROUP_WIDTH = 2048
LORA_SLOT = 256
LORA_LA_START = 128
MASK_VALUE = -1e30
DECAY_SCALE = math.exp(-0.5)


def _params(*semantics):
    return pltpu.CompilerParams(dimension_semantics=semantics, vmem_limit_bytes=VMEM_LIMIT_BYTES)


def _pick(n, candidates):
    for c in candidates:
        if n % c == 0:
            return c
    raise ValueError(f"no tile in {candidates} divides {n}")


def _split(x, terms):
    parts = []
    rest = x
    for i in range(terms):
        p = rest.astype(BF16)
        parts.append(p)
        if i + 1 < terms:
            rest = rest - p.astype(F32)
    return parts


_NN = (((1,), (0,)), ((), ()))
_NT = (((1,), (1,)), ((), ()))
_TN = (((0,), (0,)), ((), ()))


def _dotp(a, b, dims=_NN, ta=1, tb=1):
    pa = _split(a, ta)
    pb = _split(b, tb)
    out = None
    for i in range(ta):
        for j in range(tb):
            if i + j >= max(ta, tb):
                continue
            t = lax.dot_general(pa[i], pb[j], dims, preferred_element_type=F32)
            out = t if out is None else out + t
    return out


def _sigmoid(x):
    return 0.5 * jnp.tanh(0.5 * x) + 0.5


def _silu(x):
    return x * _sigmoid(x)


def _mod_kernel(c_ref, w_ref, b_ref, o_ref):
    s = _silu(c_ref[...])
    o_ref[...] = _dotp(s, w_ref[...], ta=2, tb=2) + b_ref[...]


def _modulation(cvec, mod_w, mod_b):
    depth, d, n = mod_w.shape
    rows = cvec.shape[0]
    tn = _pick(n, (512, 256, 128))
    return pl.pallas_call(
        _mod_kernel,
        grid=(depth, n // tn),
        in_specs=[
            pl.BlockSpec((rows, d), lambda l, j: (0, 0)),
            pl.BlockSpec((None, d, tn), lambda l, j: (l, 0, j)),
            pl.BlockSpec((None, 1, tn), lambda l, j: (l, 0, j)),
        ],
        out_specs=pl.BlockSpec((None, rows, tn), lambda l, j: (l, 0, j)),
        out_shape=jax.ShapeDtypeStruct((depth, rows, n), F32),
        compiler_params=_params("parallel", "parallel"),
        name="modulation",
    )(cvec, mod_w, mod_b.reshape(depth, 1, n))


def _norm_rows(x, g, shift, scale):
    ms = jnp.mean(x * x, axis=-1, keepdims=True)
    y = x * lax.rsqrt(ms + NORM_EPS)
    return (y * g) * (1.0 + scale) + shift


def _norm_mod_kernel(x_ref, c_ref, g_ref, shift_ref, scale_ref, o_ref, *, ctx_blocks):
    j = pl.program_id(1)

    @pl.when(j < ctx_blocks)
    def _():
        o_ref[...] = _norm_rows(c_ref[...], g_ref[...], shift_ref[...], scale_ref[...]).astype(BF16)

    @pl.when(j >= ctx_blocks)
    def _():
        o_ref[...] = _norm_rows(x_ref[...], g_ref[...], shift_ref[...], scale_ref[...]).astype(BF16)


def _norm_mod(x, ctx, g, mod, ctx_row):
    bsz, t, d = x.shape
    tm = 256
    if ctx is None:
        ctx = x
        cb = 0
    else:
        cb = ctx.shape[1] // tm
    nblk = cb + t // tm
    rows = mod.shape[0]
    mod3 = mod.reshape(rows, 1, 3 * d)

    def mrow(b, j):
        return jnp.where(j < cb, ctx_row, b)

    return pl.pallas_call(
        functools.partial(_norm_mod_kernel, ctx_blocks=cb),
        grid=(bsz, nblk),
        in_specs=[
            pl.BlockSpec((None, tm, d), lambda b, j: (b, jnp.maximum(j - cb, 0), 0)),
            pl.BlockSpec((None, tm, d), lambda b, j: (b, jnp.minimum(j, max(cb - 1, 0)), 0)),
            pl.BlockSpec((1, d), lambda b, j: (0, 0)),
            pl.BlockSpec((None, 1, d), lambda b, j: (mrow(b, j), 0, 0)),
            pl.BlockSpec((None, 1, d), lambda b, j: (mrow(b, j), 0, 1)),
        ],
        out_specs=pl.BlockSpec((None, tm, d), lambda b, j: (b, j, 0)),
        out_shape=jax.ShapeDtypeStruct((bsz, nblk * tm, d), BF16),
        compiler_params=_params("parallel", "parallel"),
        name="norm_mod",
    )(x, ctx, g.reshape(1, d), mod3, mod3)


def _mm_kernel(a_ref, b_ref, o_ref):
    o_ref[...] = jnp.dot(a_ref[...], b_ref[...], preferred_element_type=F32)


def _matmul(a, b):
    m, k = a.shape
    n = b.shape[1]
    tm = _pick(m, (1536, 1024, 768, 512, 256))
    tn = _pick(n, (512, 256, 128))
    return pl.pallas_call(
        _mm_kernel,
        grid=(m // tm, n // tn),
        in_specs=[
            pl.BlockSpec((tm, k), lambda i, j: (i, 0)),
            pl.BlockSpec((k, tn), lambda i, j: (0, j)),
        ],
        out_specs=pl.BlockSpec((tm, tn), lambda i, j: (i, j)),
        out_shape=jax.ShapeDtypeStruct((m, n), F32),
        compiler_params=_params("parallel", "parallel"),
        name="in_proj",
    )(a, b)


def _rope(x, cos, sin_signed):
    out = []
    for h in range(x.shape[1] // A_HEAD_DIM):
        xh = x[:, h * A_HEAD_DIM:(h + 1) * A_HEAD_DIM]
        out.append(xh * cos + pltpu.roll(xh, A_HEAD_DIM // 2, 1) * sin_signed)
    return out


def _rope_head_order(w, n_heads):
    w5 = w.reshape(w.shape[0], n_heads, 2, 2, A_HEAD_DIM // 4)
    return jnp.swapaxes(w5, 2, 3).reshape(w.shape)


def _attn_kernel(sink_ref, q_ref, g_ref, kp_ref, kc_ref, kn_ref, vp_ref, vc_ref, vn_ref,
                 kx_ref, vx_ref, cq_ref, sq_ref, cp_ref, sp_ref, cn_ref, sn_ref, o_ref,
                 *, n_blocks, kv_heads, group):
    n = pl.program_id(1)
    blk = A_BLOCK
    lctx = kx_ref.shape[0]
    scale = A_HEAD_DIM ** -0.5

    q_heads = _rope(q_ref[...], cq_ref[...], sq_ref[...])
    kp = _rope(kp_ref[...], cp_ref[...], sp_ref[...])
    kc = _rope(kc_ref[...], cq_ref[...], sq_ref[...])
    kn = _rope(kn_ref[...], cn_ref[...], sn_ref[...])

    qi = lax.broadcasted_iota(jnp.int32, (blk, lctx + 3 * blk), 0)
    kj = lax.broadcasted_iota(jnp.int32, (blk, lctx + 3 * blk), 1) - lctx
    no_prev = jnp.where(n > 0, 0, blk)
    no_next = jnp.where(n < n_blocks - 1, 0, blk)
    prev_ok = (kj >= qi + no_prev) & (kj < blk)
    next_ok = (kj - 2 * blk <= qi - no_next) & (kj >= 2 * blk)
    valid = (kj < 0) | prev_ok | ((kj >= blk) & (kj < 2 * blk)) | next_ok
    mask = jnp.where(valid, 0.0, MASK_VALUE).astype(F32)
    mask = jnp.concatenate([mask] * group, axis=0)

    gate = g_ref[...]
    scores, sinks, probs, dens = [], [], [], []
    for hk in range(kv_heads):
        sl = slice(hk * A_HEAD_DIM, (hk + 1) * A_HEAD_DIM)
        kcat = jnp.concatenate([kx_ref[:, sl], kp[hk], kc[hk], kn[hk]], axis=0).astype(BF16)
        qs = jnp.concatenate([q_heads[hk * group + g] * scale for g in range(group)], axis=0).astype(BF16)
        scores.append(lax.dot_general(qs, kcat, _NT, preferred_element_type=F32) + mask)
        sinks.append(jnp.concatenate(
            [jnp.full((blk, 1), sink_ref[hk * group + g], F32) for g in range(group)], axis=0))
    for hk in range(kv_heads):
        m = jnp.maximum(jnp.max(scores[hk], axis=-1, keepdims=True), sinks[hk])
        p = jnp.exp(scores[hk] - m)
        dens.append(jnp.exp(sinks[hk] - m) + jnp.sum(p, axis=-1, keepdims=True))
        probs.append(p.astype(BF16))
    for hk in range(kv_heads):
        sl = slice(hk * A_HEAD_DIM, (hk + 1) * A_HEAD_DIM)
        vcat = jnp.concatenate([vx_ref[:, sl], vp_ref[:, sl], vc_ref[:, sl], vn_ref[:, sl]], axis=0).astype(BF16)
        o = jnp.dot(probs[hk], vcat, preferred_element_type=F32) / dens[hk]
        for g in range(group):
            h = hk * group + g
            hs = slice(h * A_HEAD_DIM, (h + 1) * A_HEAD_DIM)
            o_ref[:, hs] = (o[g * blk:(g + 1) * blk] * _silu(gate[:, hs])).astype(BF16)


def _attention(za, zb, sink, tables, lay, n_ctx, n_lat):
    bsz = za.shape[0]
    aq, akv = lay["aq"], lay["akv"]
    heads = aq // A_HEAD_DIM
    kv_heads = akv // A_HEAD_DIM
    blk = A_BLOCK
    nb = n_lat // blk
    off = n_ctx // blk
    kcol = lay["k"] // akv
    vcol = lay["v"] // akv
    cos, sin = tables

    def prev(n):
        return jnp.maximum(n - 1, 0)

    def nxt(n):
        return jnp.minimum(n + 1, nb - 1)

    def zspec(width, rowf, col):
        return pl.BlockSpec((None, blk, width), lambda b, n: (b, rowf(n) + off, col))

    def tspec(rowf):
        return pl.BlockSpec((blk, A_HEAD_DIM), lambda b, n: (rowf(n), 0))

    ident = lambda n: n
    return pl.pallas_call(
        functools.partial(_attn_kernel, n_blocks=nb, kv_heads=kv_heads, group=heads // kv_heads),
        grid=(bsz, nb),
        in_specs=[
            pl.BlockSpec(memory_space=pltpu.SMEM),
            zspec(aq, ident, lay["q"] // aq),
            zspec(aq, ident, lay["ga"] // aq),
            zspec(akv, prev, kcol), zspec(akv, ident, kcol), zspec(akv, nxt, kcol),
            zspec(akv, prev, vcol), zspec(akv, ident, vcol), zspec(akv, nxt, vcol),
            pl.BlockSpec((None, n_ctx, akv), lambda b, n: (b, 0, kcol)),
            pl.BlockSpec((None, n_ctx, akv), lambda b, n: (b, 0, vcol)),
            tspec(ident), tspec(ident), tspec(prev), tspec(prev), tspec(nxt), tspec(nxt),
        ],
        out_specs=pl.BlockSpec((None, blk, aq), lambda b, n: (b, n, 0)),
        out_shape=jax.ShapeDtypeStruct((bsz, n_lat, aq), BF16),
        compiler_params=_params("parallel", "parallel"),
        name="band_attention",
    )(sink, za, zb, za, za, za, za, za, za, za, za, cos, sin, cos, sin, cos, sin)


def _rope_tables(n_tok):
    rows = n_tok // GRID_W
    row = jnp.repeat(jnp.arange(rows, dtype=F32), GRID_W)
    col = jnp.tile(jnp.arange(GRID_W, dtype=F32), rows)
    n_freq = A_HEAD_DIM // 4
    inv_freq = ROPE_THETA ** (-jnp.arange(n_freq, dtype=F32) / n_freq)
    ar = row[:, None] * inv_freq
    ac = col[:, None] * inv_freq
    cos = jnp.concatenate([jnp.cos(ar), jnp.cos(ac), jnp.cos(ar), jnp.cos(ac)], axis=-1)
    sin = jnp.concatenate([-jnp.sin(ar), -jnp.sin(ac), jnp.sin(ar), jnp.sin(ac)], axis=-1)
    return cos, sin


def _head_sum(x, ones_bd):
    out = []
    for i in range(x.shape[1] // LANES):
        out.append(_dotp(x[:, i * LANES:(i + 1) * LANES], ones_bd, ta=2, tb=1))
    return jnp.concatenate(out, axis=1) if len(out) > 1 else out[0]


def _head_ones():
    r = lax.broadcasted_iota(jnp.int32, (LANES, LANES), 0) // B_HEAD_DIM
    c = lax.broadcasted_iota(jnp.int32, (LANES, LANES), 1) // B_HEAD_DIM
    return (r == c).astype(F32)


def _prep_kernel(zr_ref, zk_ref, zv_ref, zrp_ref, zkp_ref, zvp_ref, zrn_ref, zkn_ref, zvn_ref,
                 zwa_ref, cwr_ref, cwk_ref, cwv_ref, w2_ref, a2_ref,
                 w0_ref, a0_ref, kk_ref, ka_ref, rk_ref,
                 r_o, v_o, kk_o, bonus_o, lw_o, kd_o, kb_o, *, ctx_blocks, n_blocks):
    j = pl.program_id(2)
    tm = zr_ref.shape[0]
    has_prev = jnp.logical_and(j != 0, j != ctx_blocks)
    has_next = jnp.logical_and(j != ctx_blocks - 1, j != n_blocks - 1)
    rows = lax.broadcasted_iota(jnp.int32, zr_ref.shape, 0)

    def conv(x_ref, p_ref, n_ref, w_ref):
        x = x_ref[...]
        w = w_ref[...]
        before = jnp.where(has_prev, p_ref[SUBLANES - 1:SUBLANES, :], 0.0)
        after = jnp.where(has_next, n_ref[0:1, :], 0.0)
        xm = jnp.where(rows == 0, before, pltpu.roll(x, 1, 0))
        xp = jnp.where(rows == tm - 1, after, pltpu.roll(x, tm - 1, 0))
        return xm * w[0:1] + x * w[1:2] + xp * w[2:3]

    r = conv(zr_ref, zrp_ref, zrn_ref, cwr_ref)
    k = conv(zk_ref, zkp_ref, zkn_ref, cwk_ref)
    v = conv(zv_ref, zvp_ref, zvn_ref, cwv_ref)
    ones_bd = _head_ones()

    kx = k * kk_ref[...]
    kkn = kx / jnp.maximum(jnp.sqrt(_head_sum(kx * kx, ones_bd)), 1e-12)
    r_o[...] = r.astype(BF16)
    v_o[...] = v.astype(BF16)
    kk_o[...] = kkn.astype(BF16)
    bonus_o[...] = _head_sum(r * k * rk_ref[...], ones_bd) * v

    zwa = zwa_ref[...]
    lw = _split(jnp.tanh(zwa[:, :LORA_SLOT]), 2)
    la = _split(zwa[:, LORA_LA_START:LORA_LA_START + LORA_SLOT], 2)
    ka = ka_ref[...]

    def lora(x, w_ref, d):
        out = jnp.dot(x[0], w_ref[d, 0], preferred_element_type=F32)
        out = out + jnp.dot(x[0], w_ref[d, 1], preferred_element_type=F32)
        return out + jnp.dot(x[1], w_ref[d, 0], preferred_element_type=F32)

    for d in range(2):
        xw = w0_ref[d:d + 1, :] + lora(lw, w2_ref, d)
        lw_o[d] = -DECAY_SCALE * _sigmoid(xw)
        a = _sigmoid(a0_ref[d:d + 1, :] + lora(la, a2_ref, d))
        kd_o[d] = (k * (1.0 + (a - 1.0) * ka)).astype(BF16)
        kb_o[d] = (kkn * a).astype(BF16)


def _rwkv_prep(z, lay, n_ctx, conv_w, w0, w2, a0, a2, k_k, k_a, r_k):
    bsz, t_all, _ = z.shape
    bw = lay["bw"]
    tm = 256
    cw = _pick(bw, (512, 256, 128))
    nblk = t_all // tm
    cb = n_ctx // tm
    sub = tm // SUBLANES
    last8 = t_all // SUBLANES - 1

    def main(off):
        return pl.BlockSpec((None, tm, cw), lambda c, b, j: (b, j, off // cw + c))

    def before(off):
        return pl.BlockSpec((None, SUBLANES, cw), lambda c, b, j: (b, jnp.maximum(j * sub - 1, 0), off // cw + c))

    def after(off):
        return pl.BlockSpec((None, SUBLANES, cw), lambda c, b, j: (b, jnp.minimum((j + 1) * sub, last8), off // cw + c))

    def chan(rows, off=0):
        return pl.BlockSpec((rows, cw), lambda c, b, j: (0, off // cw + c))

    def pad_lora(w, first_row):
        full = jnp.zeros((2, LORA_SLOT, bw), F32)
        for d in range(2):
            full = full.at[d, first_row + d * LORA_RANK:first_row + (d + 1) * LORA_RANK].set(w[d])
        hi = full.astype(BF16)
        lo = (full - hi.astype(F32)).astype(BF16)
        return jnp.stack([hi, lo], axis=1)

    lora_spec = pl.BlockSpec((2, 2, LORA_SLOT, cw), lambda c, b, j: (0, 0, 0, c))
    tok = pl.BlockSpec((None, tm, cw), lambda c, b, j: (b, j, c))
    tok2 = pl.BlockSpec((2, None, tm, cw), lambda c, b, j: (0, b, j, c))
    one = jax.ShapeDtypeStruct((bsz, t_all, bw), BF16)
    one32 = jax.ShapeDtypeStruct((bsz, t_all, bw), F32)
    two = jax.ShapeDtypeStruct((2, bsz, t_all, bw), BF16)
    two32 = jax.ShapeDtypeStruct((2, bsz, t_all, bw), F32)
    return pl.pallas_call(
        functools.partial(_prep_kernel, ctx_blocks=cb, n_blocks=nblk),
        grid=(bw // cw, bsz, nblk),
        in_specs=[
            main(lay["r"]), main(lay["kb"]), main(lay["vb"]),
            before(lay["r"]), before(lay["kb"]), before(lay["vb"]),
            after(lay["r"]), after(lay["kb"]), after(lay["vb"]),
            pl.BlockSpec((None, tm, 2 * LORA_SLOT), lambda c, b, j: (b, j, lay["lora"] // (2 * LORA_SLOT))),
            chan(3, 0), chan(3, bw), chan(3, 2 * bw),
            lora_spec, lora_spec,
            chan(2), chan(2), chan(1), chan(1), chan(1),
        ],
        out_specs=[tok, tok, tok, tok, tok2, tok2, tok2],
        out_shape=[one, one, one, one32, two32, two, two],
        compiler_params=_params("parallel", "parallel", "parallel"),
        name="rwkv_prep",
    )(z, z, z, z, z, z, z, z, z, z, conv_w, conv_w, conv_w,
      pad_lora(w2, 0), pad_lora(a2, 2 * LORA_RANK - LORA_LA_START),
      w0, a0, k_k.reshape(1, bw), k_a.reshape(1, bw), r_k.reshape(1, bw))


def _stack_heads(x, lane_is_first):
    return jnp.concatenate([jnp.where(lane_is_first, x, 0.0), jnp.where(lane_is_first, 0.0, x)], axis=0)


def _wkv_operands(r, v, kk, lw, kd, kb, reverse):
    nb, ch = SCAN_BLOCK, SCAN_CHUNK
    row = lax.broadcasted_iota(jnp.int32, (nb, nb), 0)
    col = lax.broadcasted_iota(jnp.int32, (nb, nb), 1)
    incl = ((row <= col) if reverse else (row >= col)) & ((row // ch) == (col // ch))
    cum = _dotp(incl.astype(F32), lw, ta=1, tb=3)
    lasts = [cum[i * ch:i * ch + 1, :] if reverse else cum[(i + 1) * ch - 1:(i + 1) * ch, :]
             for i in range(nb // ch)]
    last = jnp.concatenate([jnp.broadcast_to(l, (ch, l.shape[1])) for l in lasts], axis=0)
    g_in = jnp.exp(cum)
    g_inv = jnp.exp(-cum)
    g_ex = jnp.exp(cum - lw)
    g_rem = jnp.exp(last - cum)
    return dict(a=-kk * g_ex, r=r * g_in, b=kb * g_inv, k=kd * g_inv, v=v, kg=kd * g_rem, bg=kb * g_rem,
                g_end=[jnp.exp(l) for l in lasts])


def _wkv_step(streams, terms):
    nb, ch = SCAN_BLOCK, SCAN_CHUNK
    two = 2 * ch
    assert 2 * two == LANES
    dot = functools.partial(_dotp, ta=terms, tb=terms)
    ti = lax.broadcasted_iota(jnp.int32, (two, LANES), 0) % ch
    tj = lax.broadcasted_iota(jnp.int32, (two, LANES), 1) % ch
    lane = lax.broadcasted_iota(jnp.int32, (two, LANES), 1)
    first = lax.broadcasted_iota(jnp.int32, (ch, LANES), 1) < B_HEAD_DIM
    st = functools.partial(_stack_heads, lane_is_first=first)
    zeros = jnp.zeros((two, LANES), F32)

    chains = []
    for ops, s_ref, y_ref, reverse in streams:
        m_incl = (ti <= tj) if reverse else (ti >= tj)
        m_strict = (ti < tj) if reverse else (ti > tj)
        for p in range(ops["a"].shape[1] // LANES):
            chains.append(dict(ops=ops, sl=slice(p * LANES, (p + 1) * LANES), p=p, s_ref=s_ref, y_ref=y_ref,
                               m_incl=m_incl, m_strict=m_strict, reverse=reverse, state=s_ref[p]))

    n_chunks = nb // ch
    for step in range(n_chunks):
        for c in chains:
            i = n_chunks - 1 - step if c["reverse"] else step
            c["rows"] = slice(i * ch, (i + 1) * ch)
            c["g_end"] = c["ops"]["g_end"][i][:, c["sl"]]
            for n in ("a", "r", "b", "k", "v", "kg", "bg"):
                c[n] = st(c["ops"][n][c["rows"], c["sl"]])
        for c in chains:
            lhs = jnp.concatenate([c["a"], c["r"]], axis=0)
            res = dot(lhs, jnp.concatenate([c["b"], c["k"], c["state"]], axis=0), _NT)
            c["d_a"] = jnp.where(c["m_strict"], res[:two, :LANES], 0.0)
            c["d_r"] = jnp.where(c["m_incl"], res[two:, :LANES], 0.0)
            c["from_state"] = res[:, LANES:]
        for c in chains:
            c["x"] = c["from_state"][:two] + dot(c["d_a"], jnp.concatenate([zeros, c["v"]], axis=0))
            c["d"] = jnp.where(lane < two, c["d_a"], 0.0)
        for lvl in range(SCAN_LEVELS):
            for c in chains:
                if lvl + 1 < SCAN_LEVELS:
                    rhs = jnp.concatenate([jnp.concatenate([c["d"], c["x"]], axis=1),
                                           jnp.zeros((two, 2 * LANES), F32)], axis=0)
                    res = dot(c["d"], rhs)
                    c["d"] = res[:, :LANES]
                    c["x"] = c["x"] + res[:, LANES:]
                else:
                    c["x"] = c["x"] + dot(c["d"], jnp.concatenate([c["x"], zeros], axis=0))
        for c in chains:
            uv = jnp.concatenate([c["x"], c["v"]], axis=0)
            y_s = c["from_state"][two:] + dot(c["d_r"], uv)
            c["y_ref"][c["rows"], c["sl"]] = (y_s[:ch] + y_s[ch:]).astype(BF16)
            c["state"] = c["state"] * c["g_end"] + dot(uv, jnp.concatenate([c["bg"], c["kg"]], axis=0), _TN)
    for c in chains:
        c["s_ref"][c["p"]] = c["state"]


def _wkv_kernel(rf_ref, vf_ref, kkf_ref, lwf_ref, kdf_ref, kbf_ref,
                rb_ref, vb_ref, kkb_ref, lwb_ref, kdb_ref, kbb_ref,
                yf_ref, yb_ref, sf_ref, sb_ref, *, terms):
    @pl.when(pl.program_id(2) == 0)
    def _():
        sf_ref[...] = jnp.zeros_like(sf_ref)
        sb_ref[...] = jnp.zeros_like(sb_ref)

    def f32(ref):
        return ref[...].astype(F32)

    fwd = _wkv_operands(f32(rf_ref), f32(vf_ref), f32(kkf_ref), lwf_ref[...], f32(kdf_ref), f32(kbf_ref), False)
    bwd = _wkv_operands(f32(rb_ref), f32(vb_ref), f32(kkb_ref), lwb_ref[...], f32(kdb_ref), f32(kbb_ref), True)
    _wkv_step([(fwd, sf_ref, yf_ref, False), (bwd, sb_ref, yb_ref, True)], terms)


def _wkv(r, v, kk, lw, kd, kb, n_ctx, terms):
    bsz, t_all, bw = r.shape
    ch = SCAN_BLOCK
    gw = _pick(bw, (SCAN_GROUP_WIDTH, 256, 128))
    nc = t_all // ch
    ncc = n_ctx // ch

    def fwd(c):
        return c

    def bwd(c):
        return jnp.where(c < ncc, ncc - 1 - c, nc + ncc - 1 - c)

    def tok(order):
        return pl.BlockSpec((None, ch, gw), lambda b, g, c: (b, order(c), g))

    def tok2(order, d):
        return pl.BlockSpec((None, None, ch, gw), lambda b, g, c: (d, b, order(c), g))

    out = jax.ShapeDtypeStruct((bsz, t_all, bw), BF16)
    state = pltpu.VMEM((gw // LANES, LANES, LANES), F32)
    return pl.pallas_call(
        functools.partial(_wkv_kernel, terms=terms),
        grid=(bsz, bw // gw, nc),
        in_specs=[tok(fwd), tok(fwd), tok(fwd), tok2(fwd, 0), tok2(fwd, 0), tok2(fwd, 0),
                  tok(bwd), tok(bwd), tok(bwd), tok2(bwd, 1), tok2(bwd, 1), tok2(bwd, 1)],
        out_specs=[tok(fwd), tok(bwd)],
        out_shape=[out, out],
        scratch_shapes=[state, state],
        compiler_params=_params("parallel", "parallel", "arbitrary"),
        name="wkv_scan",
    )(r, v, kk, lw, kd, kb, r, v, kk, lw, kd, kb)


def _readout_kernel(yf_ref, yb_ref, bonus_ref, gate_ref, gw_ref, gb_ref, o_ref):
    ones_bd = _head_ones()
    y = yf_ref[...].astype(F32) + yb_ref[...].astype(F32)
    inv = 1.0 / B_HEAD_DIM
    mean = _head_sum(y, ones_bd) * inv
    yc = y - mean
    var = _head_sum(yc * yc, ones_bd) * inv
    out = yc * lax.rsqrt(var + GN_EPS) * gw_ref[...] + gb_ref[...]
    out = out + bonus_ref[...]
    o_ref[...] = (out * _silu(gate_ref[...])).astype(BF16)


def _readout(yf, yb, bonus, z, lay, n_ctx, gn_w, gn_b):
    bsz, t_all, bw = yf.shape
    n_lat = t_all - n_ctx
    tm = 256
    cw = _pick(bw, (2048, 512, 256, 128))
    off = n_ctx // tm
    tok = pl.BlockSpec((None, tm, cw), lambda c, b, j: (b, j + off, c))
    chan = pl.BlockSpec((1, cw), lambda c, b, j: (0, c))
    return pl.pallas_call(
        _readout_kernel,
        grid=(bw // cw, bsz, n_lat // tm),
        in_specs=[tok, tok, tok,
                  pl.BlockSpec((None, tm, cw), lambda c, b, j: (b, j + off, lay["gb"] // cw + c)),
                  chan, chan],
        out_specs=pl.BlockSpec((None, tm, cw), lambda c, b, j: (b, j, c)),
        out_shape=jax.ShapeDtypeStruct((bsz, n_lat, bw), BF16),
        compiler_params=_params("parallel", "parallel", "parallel"),
        name="rwkv_readout",
    )(yf, yb, bonus, z, gn_w.reshape(1, bw), gn_b.reshape(1, bw))


def _out_proj_kernel(a_ref, b_ref, wa_ref, wb_ref, x_ref, gate_ref, o_ref):
    y = jnp.dot(a_ref[...], wa_ref[...], preferred_element_type=F32)
    y = y + jnp.dot(b_ref[...], wb_ref[...], preferred_element_type=F32)
    o_ref[...] = x_ref[...] + gate_ref[...] * y


def _out_proj(mix_a, mix_b, w_out, x, mod):
    bsz, t, d = x.shape
    ka, kb = mix_a.shape[2], mix_b.shape[2]
    assert ka % kb == 0 and w_out.shape[0] == ka + kb
    tm = _pick(t, (1024, 512, 256))
    tn = _pick(d, (512, 256, 128))
    mod3 = mod.reshape(mod.shape[0], 1, 3 * d)
    return pl.pallas_call(
        _out_proj_kernel,
        grid=(bsz, t // tm, d // tn),
        in_specs=[
            pl.BlockSpec((None, tm, ka), lambda b, i, j: (b, i, 0)),
            pl.BlockSpec((None, tm, kb), lambda b, i, j: (b, i, 0)),
            pl.BlockSpec((ka, tn), lambda b, i, j: (0, j)),
            pl.BlockSpec((kb, tn), lambda b, i, j: (ka // kb, j)),
            pl.BlockSpec((None, tm, tn), lambda b, i, j: (b, i, j)),
            pl.BlockSpec((None, 1, tn), lambda b, i, j: (b, 0, 2 * (d // tn) + j)),
        ],
        out_specs=pl.BlockSpec((None, tm, tn), lambda b, i, j: (b, i, j)),
        out_shape=jax.ShapeDtypeStruct((bsz, t, d), F32),
        compiler_params=_params("parallel", "parallel", "parallel"),
        name="out_proj",
    )(mix_a, mix_b, w_out, w_out, x, mod3)


def _fold_lanes(x):
    out = x[:, :LANES]
    for i in range(1, x.shape[1] // LANES):
        out = out + x[:, i * LANES:(i + 1) * LANES]
    return out


def _gm_in_kernel(h_ref, wu_ref, wv_ref, wg_ref, p_ref, gv_ref, s1_ref, s2_ref):
    @pl.when(pl.program_id(1) == 0)
    def _():
        s1_ref[...] = jnp.zeros_like(s1_ref)
        s2_ref[...] = jnp.zeros_like(s2_ref)

    h = h_ref[...]
    gv = jax.nn.gelu(jnp.dot(h, wv_ref[...], preferred_element_type=F32))
    gv_ref[...] = gv.astype(BF16)
    s1_ref[...] += _fold_lanes(gv)
    s2_ref[...] += _fold_lanes(gv * gv)
    gu = jax.nn.gelu(jnp.dot(h, wu_ref[...], preferred_element_type=F32))
    g = jnp.dot(h, wg_ref[...], preferred_element_type=F32)
    p_ref[...] = (gu * _silu(g)).astype(BF16)


def _gm_in(h, w_in, cwid):
    m, d = h.shape
    tm = _pick(m, (1024, 512, 256))
    tn = _pick(cwid, (512, 256, 128))
    nt = cwid // tn
    stat = jax.ShapeDtypeStruct((m, LANES), F32)
    return pl.pallas_call(
        _gm_in_kernel,
        grid=(m // tm, nt),
        in_specs=[
            pl.BlockSpec((tm, d), lambda i, j: (i, 0), pipeline_mode=pl.Buffered(1)),
            pl.BlockSpec((d, tn), lambda i, j: (0, j)),
            pl.BlockSpec((d, tn), lambda i, j: (0, nt + j)),
            pl.BlockSpec((d, tn), lambda i, j: (0, 2 * nt + j)),
        ],
        out_specs=[
            pl.BlockSpec((tm, tn), lambda i, j: (i, j)),
            pl.BlockSpec((tm, tn), lambda i, j: (i, j)),
            pl.BlockSpec((tm, LANES), lambda i, j: (i, 0)),
            pl.BlockSpec((tm, LANES), lambda i, j: (i, 0)),
        ],
        out_shape=[jax.ShapeDtypeStruct((m, cwid), BF16), jax.ShapeDtypeStruct((m, cwid), BF16), stat, stat],
        compiler_params=_params("parallel", "arbitrary"),
        name="gmlp_in",
    )(h, w_in, w_in, w_in)


def _gm_out_kernel(gv_ref, p_ref, s1_ref, s2_ref, lng_ref, lnb_ref, ws_ref, bs_ref, wo_ref, x_ref, gate_ref,
                   o_ref, y_ref, *, cwid, groups):
    s = pl.program_id(0)
    j = pl.program_id(1)
    tm = gv_ref.shape[0]

    @pl.when(jnp.logical_and(s == 0, j == 0))
    def _():
        y_ref[...] = jnp.zeros_like(y_ref)

    prev = (s + 1) % 2
    acc = jnp.dot(y_ref[prev, 0], wo_ref[0], preferred_element_type=F32)
    for g in range(1, groups):
        acc = acc + jnp.dot(y_ref[prev, g], wo_ref[g], preferred_element_type=F32)
    o_ref[...] = x_ref[...] + gate_ref[...] * acc

    inv = 1.0 / cwid
    mean = jnp.sum(s1_ref[...], axis=-1, keepdims=True) * inv
    var = jnp.sum(s2_ref[...], axis=-1, keepdims=True) * inv - mean * mean
    rstd = lax.rsqrt(var + LN_EPS)
    vln = ((gv_ref[...].astype(F32) - mean) * rstd * lng_ref[j] + lnb_ref[j]).astype(BF16)
    ws = ws_ref[j].astype(BF16)
    bias = bs_ref[j]
    p = p_ref[...]
    ys = []
    for c in range(tm // C_CHUNK):
        rows = slice(c * C_CHUNK, (c + 1) * C_CHUNK)
        vm = jnp.dot(ws, vln[rows], preferred_element_type=F32) + bias
        ys.append((p[rows].astype(F32) * vm).astype(BF16))
    y_ref[s % 2, j] = jnp.concatenate(ys, axis=0)


def _gm_out(gv, p, s1, s2, ln_g, ln_b, w_s, b_s, w_out, x, mod):
    m, cwid = gv.shape
    groups = w_s.shape[0]
    gwid = cwid // groups
    d = w_out.shape[1]
    tm = _pick(m, (512, 256))
    nrb = m // tm
    assert d % groups == 0 and (d // groups) % LANES == 0 and x.shape[1] % tm == 0
    tn = d // groups
    rb_per_batch = x.shape[1] // tm

    def row(s):
        return jnp.minimum(s, nrb - 1)

    def out_idx(s, j):
        return (jnp.maximum(s - 1, 0), jnp.where(s == 0, 0, j))

    return pl.pallas_call(
        functools.partial(_gm_out_kernel, cwid=cwid, groups=groups),
        grid=(nrb + 1, groups),
        in_specs=[
            pl.BlockSpec((tm, gwid), lambda s, j: (row(s), j)),
            pl.BlockSpec((tm, gwid), lambda s, j: (row(s), j)),
            pl.BlockSpec((tm, LANES), lambda s, j: (row(s), 0)),
            pl.BlockSpec((tm, LANES), lambda s, j: (row(s), 0)),
            pl.BlockSpec((groups, 1, gwid), lambda s, j: (0, 0, 0)),
            pl.BlockSpec((groups, 1, gwid), lambda s, j: (0, 0, 0)),
            pl.BlockSpec((groups, C_CHUNK, C_CHUNK), lambda s, j: (0, 0, 0)),
            pl.BlockSpec((groups, C_CHUNK, 1), lambda s, j: (0, 0, 0)),
            pl.BlockSpec((groups, gwid, tn), lambda s, j: (0, 0, j)),
            pl.BlockSpec((tm, tn), out_idx),
            pl.BlockSpec((None, 1, tn), lambda s, j: (out_idx(s, j)[0] // rb_per_batch, 0, 2 * groups + out_idx(s, j)[1])),
        ],
        out_specs=pl.BlockSpec((tm, tn), out_idx),
        out_shape=jax.ShapeDtypeStruct((m, d), F32),
        scratch_shapes=[pltpu.VMEM((2, groups, tm, gwid), BF16)],
        compiler_params=_params("arbitrary", "arbitrary"),
        name="gmlp_out",
    )(gv, p, s1, s2, ln_g.reshape(groups, 1, gwid), ln_b.reshape(groups, 1, gwid), w_s,
      b_s.reshape(groups, C_CHUNK, 1), w_out.reshape(groups, gwid, d), x.reshape(m, d),
      mod.reshape(mod.shape[0], 1, 3 * d))


def _final_kernel(x_ref, g_ref, o_ref):
    x = x_ref[...]
    ms = jnp.mean(x * x, axis=-1, keepdims=True)
    o_ref[...] = (x * lax.rsqrt(ms + NORM_EPS)) * g_ref[...]


def _final(x, final_g):
    m, d = x.shape
    tm = 256
    tok = pl.BlockSpec((tm, d), lambda i: (i, 0))
    return pl.pallas_call(
        _final_kernel,
        grid=(m // tm,),
        in_specs=[tok, pl.BlockSpec((1, d), lambda i: (0, 0))],
        out_specs=tok,
        out_shape=jax.ShapeDtypeStruct((m, d), F32),
        compiler_params=_params("parallel"),
        name="final_norm",
    )(x, final_g.reshape(1, d))


def _ab_layout(d, heads, ab_in):
    aq = heads * A_HEAD_DIM
    bw = d // 2
    akv = (ab_in - 2 * aq - 4 * bw - 4 * LORA_RANK) // 2
    lay = {"aq": aq, "akv": akv, "bw": bw}
    lay.update(q=0, k=aq, v=aq + akv, width_a=aq + 2 * akv)
    off = 0
    for name, width in (("ga", aq), ("r", bw), ("kb", bw), ("vb", bw), ("gb", bw), ("lora", 2 * LORA_SLOT)):
        lay[name] = off
        off += width
    lay["width_b"] = off
    return lay


def _ab_weights(w_in, lay):
    aq, akv = lay["aq"], lay["akv"]
    qk = _rope_head_order(w_in[:, :aq + akv], (aq + akv) // A_HEAD_DIM)
    w_a = jnp.concatenate([qk, w_in[:, aq + akv:lay["width_a"]]], axis=1).astype(BF16)
    rest = w_in[:, lay["width_a"]:]
    w_b = jnp.pad(rest, ((0, 0), (0, lay["width_b"] - rest.shape[1]))).astype(BF16)
    return w_a, w_b


def kernel(x, c, ctx, c_ctx, mod_w, mod_b, norm_g, ab_w_in, ab_w_out, attn_sink, rwkv_conv, rwkv_w0, rwkv_w2,
           rwkv_a0, rwkv_a2, rwkv_k_k, rwkv_k_a, rwkv_r_k, rwkv_gn_w, rwkv_gn_b, gm_w_in, gm_ln_g, gm_ln_b,
           gm_w_s, gm_b_s, gm_w_out, final_g):
    bsz, n_lat, d = x.shape
    n_ctx = ctx.shape[1]
    heads = attn_sink.shape[1]

    cvec = jnp.concatenate([c, c_ctx[None, :], jnp.zeros((SUBLANES - bsz - 1, d), F32)], axis=0)
    mod = _modulation(cvec, mod_w, mod_b)

    lay = _ab_layout(d, heads, ab_w_in.shape[2])
    w_a, w_b = _ab_weights(ab_w_in[0], lay)
    t_all = n_ctx + n_lat
    h0 = _norm_mod(x, ctx, norm_g[0], mod[0], bsz).reshape(bsz * t_all, d)
    za = _matmul(h0, w_a).reshape(bsz, t_all, lay["width_a"])
    zb = _matmul(h0, w_b).reshape(bsz, t_all, lay["width_b"])
    mix_a = _attention(za, zb, attn_sink[0], _rope_tables(n_lat), lay, n_ctx, n_lat)
    r, v, kk, bonus, lw, kd, kb = _rwkv_prep(zb, lay, n_ctx, rwkv_conv[0], rwkv_w0[0], rwkv_w2[0], rwkv_a0[0],
                                             rwkv_a2[0], rwkv_k_k[0], rwkv_k_a[0], rwkv_r_k[0])
    yf, yb = _wkv(r, v, kk, lw, kd, kb, n_ctx, terms=1)
    mix_b = _readout(yf, yb, bonus, zb, lay, n_ctx, rwkv_gn_w[0], rwkv_gn_b[0])
    x1 = _out_proj(mix_a, mix_b, ab_w_out[0].astype(BF16), x, mod[0])

    cwid = gm_ln_g.shape[1]
    h1 = _norm_mod(x1, None, norm_g[1], mod[1], 0)
    p, gv, s1, s2 = _gm_in(h1.reshape(bsz * n_lat, d), gm_w_in[0].astype(BF16), cwid)
    x2 = _gm_out(gv, p, s1, s2, gm_ln_g[0], gm_ln_b[0], gm_w_s[0], gm_b_s[0], gm_w_out[0].astype(BF16), x1, mod[1])
    return _final(x2, final_g).reshape(bsz, n_lat, d)
```
